```python
import math
import jax, jax.numpy as jnp
from jax import lax
import numpy as np

D_MODEL = 1024
BATCH = 8
SEQ = 2048
DEPTH = 1
DEC_BATCH = 128
DEC_SEQ = 1
PAST_LEN = 16384
PAGE_SIZE = 128

D_CONV_A = D_MODEL
K_A = 3
EXPAND = 2
D_INNER = EXPAND * D_MODEL
HEAD_DIM = 64
N_HEADS = D_INNER // HEAD_DIM
N_GROUPS = 4
HEADS_PER_GROUP = N_HEADS // N_GROUPS
D_STATE = 128
K_B = 4
CHUNK = 128
D_XBC = D_INNER + 2 * N_GROUPS * D_STATE
D_FF = 4 * D_MODEL
EPS = 1e-6

_SECTION_WIDTHS = [D_CONV_A, D_CONV_A, D_CONV_A, D_INNER, D_XBC, N_HEADS, D_MODEL, D_MODEL]
SPLIT_POINTS = [int(v) for v in np.cumsum(_SECTION_WIDTHS)[:-1]]
D_IN_PROJ = int(sum(_SECTION_WIDTHS))

kernel_name = 'hybrid_shortconv_ssd_decoder_step'


def rmsnorm(x, g):
    xf = x.astype(jnp.float32)
    ms = jnp.mean(xf * xf, axis=-1, keepdims=True)
    return (xf * lax.rsqrt(ms + EPS) * g.astype(jnp.float32)).astype(x.dtype)


def group_rmsnorm(y, g):
    b, L, d = y.shape
    yf = y.astype(jnp.float32).reshape(b, L, N_GROUPS, d // N_GROUPS)
    ms = jnp.mean(yf * yf, axis=-1, keepdims=True)
    yn = (yf * lax.rsqrt(ms + EPS)).reshape(b, L, d)
    return (yn * g.astype(jnp.float32)).astype(y.dtype)


def causal_conv(u, prev, w, bias=None):
    K = w.shape[0]
    L = u.shape[1]
    full = jnp.concatenate([prev.astype(u.dtype), u], axis=1)
    out = full[:, 0:L] * w[0]
    for k in range(1, K):
        out = out + full[:, k:k + L] * w[k]
    if bias is not None:
        out = out + bias
    return out, full[:, L:]


def ssd(xh, dt, a, bm, cm, h0):
    b, L = xh.shape[0], xh.shape[1]
    q = min(CHUNK, L)
    nc = -(-L // q)
    pad = nc * q - L
    f32 = jnp.float32
    if pad:
        padt = lambda t: jnp.pad(t, [(0, 0), (0, pad)] + [(0, 0)] * (t.ndim - 2))
        xh, dt, bm, cm = padt(xh), padt(dt), padt(bm), padt(cm)
    G, R, P, N = N_GROUPS, HEADS_PER_GROUP, HEAD_DIM, D_STATE
    x = xh.reshape(b, nc, q, G, R, P).astype(f32)
    dtc = dt.reshape(b, nc, q, G, R)
    B = bm.reshape(b, nc, q, G, N).astype(f32)
    C = cm.reshape(b, nc, q, G, N).astype(f32)
    acs = jnp.cumsum(dtc * a.reshape(G, R), axis=2)
    acs_t = jnp.moveaxis(acs, 2, -1)
    seg = acs_t[..., :, None] - acs_t[..., None, :]
    causal = jnp.tril(jnp.ones((q, q), dtype=bool))
    lmat = jnp.exp(jnp.where(causal, seg, -jnp.inf))
    cb = jnp.einsum('bcqgn,bcsgn->bcgqs', C, B)
    wts = cb[:, :, :, None] * lmat * jnp.moveaxis(dtc, 2, -1)[..., None, :]
    y_diag = jnp.einsum('bcgrqs,bcsgrp->bcqgrp', wts, x)
    decay_end = jnp.exp(acs[:, :, -1:] - acs) * dtc
    states = jnp.einsum('bcsgn,bcsgr,bcsgrp->bcgrpn', B, decay_end, x)
    chunk_decay = jnp.exp(acs[:, :, -1])

    def step(h, inp):
        dec, st = inp
        return dec[..., None, None] * h + st, h

    h_last, h_in = lax.scan(step, h0.reshape(b, G, R, P, N).astype(f32),
                            (jnp.moveaxis(chunk_decay, 1, 0), jnp.moveaxis(states, 1, 0)))
    y_off = jnp.einsum('bcqgn,bcqgr,cbgrpn->bcqgrp', C, jnp.exp(acs), h_in)
    y = (y_diag + y_off).reshape(b, nc * q, N_HEADS, P)[:, :L]
    return y, h_last.reshape(b, N_HEADS, P, N)


def layer(x, c, st_a, st_bconv, st_ssm, w_ada, b_ada, norm1_g, w_in, conv_a_w, w_a_out,
          conv_b_w, conv_b_b, dt_bias, a_log, d_skip, ssm_norm_g, w_b_out, w_o,
          norm2_g, w_mlp1, w_mlp2):
    b_, L = x.shape[0], x.shape[1]
    mod = jax.nn.silu(c) @ w_ada + b_ada
    sh1, sc1, g1, sh2, sc2, g2 = jnp.split(mod[:, None, :], 6, axis=-1)
    u = rmsnorm(x, norm1_g) * (1 + sc1) + sh1
    proj = u @ w_in
    bgate, cgate, hval, z, xbc, dt_raw, ga, gb = jnp.split(proj, SPLIT_POINTS, axis=-1)
    conv_out, new_a = causal_conv(cgate * hval, st_a, conv_a_w)
    y_a = (bgate * conv_out) @ w_a_out
    xbc_c, new_bconv = causal_conv(xbc, st_bconv, conv_b_w, conv_b_b)
    xbc_c = jax.nn.silu(xbc_c)
    xs, bm, cm = jnp.split(xbc_c, [D_INNER, D_INNER + N_GROUPS * D_STATE], axis=-1)
    dt = jax.nn.softplus(dt_raw.astype(jnp.float32) + dt_bias.astype(jnp.float32))
    a = -jnp.exp(a_log.astype(jnp.float32))
    xh = xs.reshape(b_, L, N_HEADS, HEAD_DIM)
    y_s, new_ssm = ssd(xh, dt, a, bm.reshape(b_, L, N_GROUPS, D_STATE),
                       cm.reshape(b_, L, N_GROUPS, D_STATE), st_ssm)
    y_s = y_s + d_skip.astype(jnp.float32)[:, None] * xh.astype(jnp.float32)
    y_s = y_s.reshape(b_, L, D_INNER).astype(x.dtype) * jax.nn.silu(z)
    y_b = group_rmsnorm(y_s, ssm_norm_g) @ w_b_out
    merged = jax.nn.sigmoid(ga) * y_a + jax.nn.sigmoid(gb) * y_b
    x = x + g1 * (merged @ w_o)
    u2 = rmsnorm(x, norm2_g) * (1 + sc2) + sh2
    hmid = jnp.square(jax.nn.relu(u2 @ w_mlp1))
    x = x + g2 * (hmid @ w_mlp2)
    return x, new_a.astype(st_a.dtype), new_bconv.astype(st_bconv.dtype), new_ssm.astype(st_ssm.dtype)


def setup_inputs(seed: int = 0) -> dict:
    key = jax.random.key(seed)
    ks = jax.random.split(key, 32)
    f32 = jnp.float32

    def nrm(k, shape, scale):
        return jax.random.normal(k, shape, f32) * scale

    Ld = DEPTH
    dt0 = jnp.exp(jax.random.uniform(ks[10], (Ld, N_HEADS), f32, math.log(1e-3), math.log(1e-1)))
    return {
        'x_prompt': nrm(ks[0], (BATCH, SEQ, D_MODEL), 1.0),
        'x_sample': nrm(ks[1], (DEC_BATCH, DEC_SEQ, D_MODEL), 1.0),
        'c_prompt': nrm(ks[2], (BATCH, D_MODEL), 1.0),
        'c_sample': nrm(ks[3], (DEC_BATCH, D_MODEL), 1.0),
        'state_shortconv': nrm(ks[4], (Ld, DEC_BATCH, K_A - 1, D_CONV_A), 1.0),
        'state_ssm_conv': nrm(ks[5], (Ld, DEC_BATCH, K_B - 1, D_XBC), 1.0),
        'state_ssm': nrm(ks[6], (Ld, DEC_BATCH, N_HEADS, HEAD_DIM, D_STATE), 0.5),
        'w_ada': nrm(ks[7], (Ld, D_MODEL, 6 * D_MODEL), 0.5 * D_MODEL ** -0.5),
        'b_ada': nrm(ks[8], (Ld, 6 * D_MODEL), 0.02),
        'norm1_g': 1.0 + nrm(ks[9], (Ld, D_MODEL), 0.05),
        'w_in': nrm(ks[11], (Ld, D_MODEL, D_IN_PROJ), D_MODEL ** -0.5),
        'conv_a_w': nrm(ks[12], (Ld, K_A, D_CONV_A), K_A ** -0.5),
        'w_a_out': nrm(ks[13], (Ld, D_CONV_A, D_MODEL), D_CONV_A ** -0.5),
        'conv_b_w': nrm(ks[14], (Ld, K_B, D_XBC), K_B ** -0.5),
        'conv_b_b': nrm(ks[15], (Ld, D_XBC), 0.02),
        'dt_bias': dt0 + jnp.log(-jnp.expm1(-dt0)),
        'a_log': jnp.log(jax.random.uniform(ks[16], (Ld, N_HEADS), f32, 1.0, 16.0)),
        'd_skip': 1.0 + nrm(ks[17], (Ld, N_HEADS), 0.1),
        'ssm_norm_g': 1.0 + nrm(ks[18], (Ld, D_INNER), 0.05),
        'w_b_out': nrm(ks[19], (Ld, D_INNER, D_MODEL), D_INNER ** -0.5),
        'w_o': nrm(ks[20], (Ld, D_MODEL, D_MODEL), D_MODEL ** -0.5),
        'norm2_g': 1.0 + nrm(ks[21], (Ld, D_MODEL), 0.05),
        'w_mlp1': nrm(ks[22], (Ld, D_MODEL, D_FF), D_MODEL ** -0.5),
        'w_mlp2': nrm(ks[23], (Ld, D_FF, D_MODEL), D_FF ** -0.5),
        'norm_f_g': 1.0 + nrm(ks[24], (D_MODEL,), 0.05),
    }


def reference(x_prompt, x_sample, c_prompt, c_sample, state_shortconv, state_ssm_conv, state_ssm,
              w_ada, b_ada, norm1_g, w_in, conv_a_w, w_a_out, conv_b_w, conv_b_b, dt_bias, a_log,
              d_skip, ssm_norm_g, w_b_out, w_o, norm2_g, w_mlp1, w_mlp2, norm_f_g):
    yp, ys = x_prompt, x_sample
    bp = x_prompt.shape[0]
    pa, pbc, ps, sa, sbc, ss = [], [], [], [], [], []
    for l in range(DEPTH):
        lw = (w_ada[l], b_ada[l], norm1_g[l], w_in[l], conv_a_w[l], w_a_out[l], conv_b_w[l],
              conv_b_b[l], dt_bias[l], a_log[l], d_skip[l], ssm_norm_g[l], w_b_out[l], w_o[l],
              norm2_g[l], w_mlp1[l], w_mlp2[l])
        z_a = jnp.zeros((bp, K_A - 1, D_CONV_A), state_shortconv.dtype)
        z_bc = jnp.zeros((bp, K_B - 1, D_XBC), state_ssm_conv.dtype)
        z_s = jnp.zeros((bp, N_HEADS, HEAD_DIM, D_STATE), state_ssm.dtype)
        yp, na, nbc, ns = layer(yp, c_prompt, z_a, z_bc, z_s, *lw)
        pa.append(na); pbc.append(nbc); ps.append(ns)
        ys, na, nbc, ns = layer(ys, c_sample, state_shortconv[l], state_ssm_conv[l], state_ssm[l], *lw)
        sa.append(na); sbc.append(nbc); ss.append(ns)
    y_prompt = rmsnorm(yp, norm_f_g)
    y_sample = rmsnorm(ys, norm_f_g)
    return (y_prompt, y_sample, jnp.stack(pa), jnp.stack(pbc), jnp.stack(ps),
            jnp.stack(sa), jnp.stack(sbc), jnp.stack(ss))
```

```python
import functools

import jax
import jax.numpy as jnp
from jax import lax
from jax.experimental import pallas as pl
from jax.experimental.pallas import tpu as pltpu

F32 = jnp.float32
BF16 = jnp.bfloat16

EPS = 1e-6
HEAD_DIM = 64
N_GROUPS = 4
D_STATE = 128
LANES = 128
SSD_CHUNK = 128
QUAD = 4
VMEM_LIMIT = 56 * 1024 * 1024


def _full_spec(shape):
    nd = len(shape)
    return pl.BlockSpec(shape, lambda *_: (0,) * nd, pipeline_mode=pl.Buffered(1))


def _bdot(a, b):
    return jnp.dot(a.astype(BF16), b, preferred_element_type=F32)


def _dot_nt(a, b):
    return lax.dot_general(a, b, (((1,), (1,)), ((), ())), preferred_element_type=F32)


def _rms(x, g):
    ms = jnp.mean(x * x, axis=-1, keepdims=True)
    return x * lax.rsqrt(ms + EPS) * g


def _silu(x):
    return x * jax.nn.sigmoid(x)


def _softplus(x):
    return jnp.maximum(x, 0.0) + jnp.log1p(jnp.exp(-jnp.abs(x)))


def _split2(v):
    hi = v.astype(BF16)
    lo = (v - hi.astype(F32)).astype(BF16)
    return hi, lo


def _split3(v):
    hi = v.astype(BF16)
    r = v - hi.astype(F32)
    mid = r.astype(BF16)
    lo = (r - mid.astype(F32)).astype(BF16)
    return hi, mid, lo


def _expand_heads(v, e2_ref):
    hi, lo = _split2(v)
    return jnp.dot(jnp.concatenate([hi, lo], axis=1), e2_ref[...], preferred_element_type=F32)


def _group_rmsnorm(y, g):
    d = y.shape[-1]
    w = d // N_GROUPS
    parts = []
    for k in range(N_GROUPS):
        yk = y[:, k * w:(k + 1) * w]
        ms = jnp.mean(yk * yk, axis=-1, keepdims=True)
        parts.append(yk * lax.rsqrt(ms + EPS))
    return jnp.concatenate(parts, axis=1) * g


def _mlp(x, sh2, sc2, g2, n2g, w1_ref, w2_ref):
    d_ff = w1_ref.shape[1]
    blk = 1024
    ub = (_rms(x, n2g) * (1.0 + sc2) + sh2).astype(BF16)
    acc = None
    for j in range(d_ff // blk):
        h = jnp.dot(ub, w1_ref[:, j * blk:(j + 1) * blk], preferred_element_type=F32)
        h = jnp.square(jnp.maximum(h, 0.0))
        p = _bdot(h, w2_ref[j * blk:(j + 1) * blk, :])
        acc = p if acc is None else acc + p
    return x + g2 * acc


def _mod_kernel(c_ref, w_ref, b_ref, o_ref):
    c = c_ref[...]
    o_ref[...] = _bdot(_silu(c), w_ref[...]) + b_ref[...]


def _modulation(c_all, w_ada, b_ada):
    n, d = c_all.shape
    dm = w_ada.shape[1]
    blk = 1024
    return pl.pallas_call(
        _mod_kernel,
        grid=(dm // blk,),
        in_specs=[pl.BlockSpec((n, d), lambda j: (0, 0)),
                  pl.BlockSpec((d, blk), lambda j: (0, j)),
                  pl.BlockSpec((1, blk), lambda j: (0, j))],
        out_specs=pl.BlockSpec((n, blk), lambda j: (0, j)),
        out_shape=jax.ShapeDtypeStruct((n, dm), F32),
        name="modulation",
    )(c_all, w_ada, b_ada)


def _ssd_chunk(xs_c, b_c, c_c, dt_c, a_row, d_row, ht_ref, e2_ref):
    q = xs_c.shape[0]
    n = D_STATE
    gw = xs_c.shape[1] // N_GROUPS
    qw = QUAD * HEAD_DIM
    row = lax.broadcasted_iota(jnp.int32, (q, q), 0)
    col = lax.broadcasted_iota(jnp.int32, (q, q), 1)
    causal = row >= col
    tri = jnp.where(causal, 1.0, 0.0).astype(BF16)
    lane_head = lax.broadcasted_iota(jnp.int32, (q, qw), 1) // HEAD_DIM

    da = dt_c * a_row
    acs = sum(jnp.dot(tri, p, preferred_element_type=F32) for p in _split3(da))
    acs_t = acs.T
    dt_t = dt_c.T
    ea = jnp.exp(acs)
    dd = jnp.exp(acs[q - 1:q, :] - acs) * dt_c
    ea_x = _expand_heads(ea, e2_ref)
    dd_x = _expand_heads(dd, e2_ref)
    xd = (xs_c * dd_x).astype(BF16)
    bb = b_c.astype(BF16)
    cb_ = c_c.astype(BF16)

    y_parts = []
    for g in range(N_GROUPS):
        bg = bb[:, g * n:(g + 1) * n]
        cg = cb_[:, g * n:(g + 1) * n]
        cbm = _dot_nt(cg, bg)
        ht_g = ht_ref[:, g * gw:(g + 1) * gw]
        y_off = jnp.dot(cg, ht_g.astype(BF16), preferred_element_type=F32)
        for qd in range(gw // qw):
            lo = g * gw + qd * qw
            h0 = lo // HEAD_DIM
            ws = []
            for r in range(QUAD):
                h = h0 + r
                seg = acs[:, h:h + 1] - acs_t[h:h + 1, :]
                lm = jnp.exp(jnp.where(causal, seg, -jnp.inf))
                ws.append((cbm * lm * dt_t[h:h + 1, :]).astype(BF16))
            wcat = jnp.concatenate(ws, axis=1)
            xq = xs_c[:, lo:lo + qw]
            rhs = jnp.concatenate(
                [jnp.where(lane_head == r, xq, 0.0).astype(BF16) for r in range(QUAD)], axis=0)
            y_diag = jnp.dot(wcat, rhs, preferred_element_type=F32)
            y_parts.append(y_diag + y_off[:, qd * qw:(qd + 1) * qw] * ea_x[:, lo:lo + qw]
                           + d_row[:, lo:lo + qw] * xq)
        bg_t = b_c[:, g * n:(g + 1) * n].T.astype(BF16)
        st = jnp.dot(bg_t, xd[:, g * gw:(g + 1) * gw], preferred_element_type=F32)
        ht_ref[:, g * gw:(g + 1) * gw] = ht_g * ea_x[q - 1:q, g * gw:(g + 1) * gw] + st
    return jnp.concatenate(y_parts, axis=1)


def _prompt_mixer_kernel(x_ref, mod_ref, n1g_ref, w_abc_ref, w_z_ref, w_xbc_ref, w_dt_ref, w_g_ref,
                         caw_ref, cbw_ref, cbb_ref, dtb_ref, alog_ref, dexp_ref, sng_ref,
                         w_ao_ref, w_bo_ref, w_o_ref, e2_ref,
                         x1_ref, na_ref, nbc_ref, nssm_ref,
                         cbuf, xbuf, ybuf, ht_ref):
    t = pl.program_id(1)
    nt = pl.num_programs(1)
    tq = x_ref.shape[1]
    d = x_ref.shape[2]
    d_inner = w_z_ref.shape[1]
    ka = caw_ref.shape[0]
    kb = cbw_ref.shape[0]
    pad = 8

    @pl.when(t == 0)
    def _():
        cbuf[0:pad, :] = jnp.zeros((pad, cbuf.shape[1]), F32)
        xbuf[0:pad, :] = jnp.zeros((pad, xbuf.shape[1]), F32)
        ht_ref[...] = jnp.zeros(ht_ref.shape, F32)

    x = x_ref[0]
    sh1 = mod_ref[0, :, 0:d]
    sc1 = mod_ref[0, :, d:2 * d]
    g1 = mod_ref[0, :, 2 * d:3 * d]
    ub = (_rms(x, n1g_ref[...]) * (1.0 + sc1) + sh1).astype(BF16)

    bgate = jnp.dot(ub, w_abc_ref[:, 0:d], preferred_element_type=F32)
    cgate = jnp.dot(ub, w_abc_ref[:, d:2 * d], preferred_element_type=F32)
    hval = jnp.dot(ub, w_abc_ref[:, 2 * d:3 * d], preferred_element_type=F32)
    cbuf[pad:pad + tq, :] = cgate * hval
    conv = cbuf[pad - (ka - 1):pad - (ka - 1) + tq, :] * caw_ref[0:1, :]
    for k in range(1, ka):
        conv = conv + cbuf[pad - (ka - 1) + k:pad - (ka - 1) + k + tq, :] * caw_ref[k:k + 1, :]
    new_a = cbuf[pad + tq - (ka - 1):pad + tq, :]
    cbuf[pad - (ka - 1):pad, :] = new_a
    y_a = _bdot(bgate * conv, w_ao_ref[...])
    merged = jax.nn.sigmoid(jnp.dot(ub, w_g_ref[:, 0:d], preferred_element_type=F32)) * y_a

    xbuf[pad:pad + tq, :] = jnp.dot(ub, w_xbc_ref[...], preferred_element_type=F32)
    xc = xbuf[pad - (kb - 1):pad - (kb - 1) + tq, :] * cbw_ref[0:1, :]
    for k in range(1, kb):
        xc = xc + xbuf[pad - (kb - 1) + k:pad - (kb - 1) + k + tq, :] * cbw_ref[k:k + 1, :]
    xc = _silu(xc + cbb_ref[...])
    new_bc = xbuf[pad + tq - (kb - 1):pad + tq, :]
    xbuf[pad - (kb - 1):pad, :] = new_bc

    dt = _softplus(jnp.dot(ub, w_dt_ref[...], preferred_element_type=F32) + dtb_ref[...])
    a_row = -jnp.exp(alog_ref[...])
    gn = N_GROUPS * D_STATE
    for c in range(tq // SSD_CHUNK):
        s = slice(c * SSD_CHUNK, (c + 1) * SSD_CHUNK)
        ybuf[s, :] = _ssd_chunk(xc[s, 0:d_inner], xc[s, d_inner:d_inner + gn],
                                xc[s, d_inner + gn:d_inner + 2 * gn], dt[s, :], a_row,
                                dexp_ref[...], ht_ref, e2_ref)

    z = jnp.dot(ub, w_z_ref[...], preferred_element_type=F32)
    y_b = _bdot(_group_rmsnorm(ybuf[...] * _silu(z), sng_ref[...]), w_bo_ref[...])
    merged = merged + jax.nn.sigmoid(jnp.dot(ub, w_g_ref[:, d:2 * d], preferred_element_type=F32)) * y_b
    x1_ref[0] = x + g1 * _bdot(merged, w_o_ref[...])

    @pl.when(t == nt - 1)
    def _():
        na_ref[0] = new_a
        nbc_ref[0] = new_bc
        h = ht_ref[...].T
        nssm_ref[0] = h.reshape(nssm_ref.shape[1:])


def _prompt_mixer(x, mod_p, n1g, w_abc, w_z, w_xbc, w_dt, w_g, caw, cbw, cbb, dtb, alog, dexp, sng,
                  w_ao, w_bo, w_o, e2, *, tq):
    b, l, d = x.shape
    d_inner = w_z.shape[1]
    d_xbc = w_xbc.shape[1]
    n_heads = d_inner // HEAD_DIM
    consts = (n1g, w_abc, w_z, w_xbc, w_dt, w_g, caw, cbw, cbb, dtb, alog, dexp, sng, w_ao, w_bo, w_o, e2)
    return pl.pallas_call(
        _prompt_mixer_kernel,
        grid=(b, l // tq),
        in_specs=[pl.BlockSpec((1, tq, d), lambda i, t: (i, t, 0)),
                  pl.BlockSpec((1, 1, mod_p.shape[2]), lambda i, t: (i, 0, 0))]
                 + [_full_spec(c.shape) for c in consts],
        out_specs=[pl.BlockSpec((1, tq, d), lambda i, t: (i, t, 0)),
                   pl.BlockSpec((1, caw.shape[0] - 1, d), lambda i, t: (i, 0, 0)),
                   pl.BlockSpec((1, cbw.shape[0] - 1, d_xbc), lambda i, t: (i, 0, 0)),
                   pl.BlockSpec((1, n_heads, HEAD_DIM, D_STATE), lambda i, t: (i, 0, 0, 0))],
        out_shape=[jax.ShapeDtypeStruct((b, l, d), F32),
                   jax.ShapeDtypeStruct((b, caw.shape[0] - 1, d), F32),
                   jax.ShapeDtypeStruct((b, cbw.shape[0] - 1, d_xbc), F32),
                   jax.ShapeDtypeStruct((b, n_heads, HEAD_DIM, D_STATE), F32)],
        scratch_shapes=[pltpu.VMEM((tq + 8, d), F32),
                        pltpu.VMEM((tq + 8, d_xbc), F32),
                        pltpu.VMEM((tq, d_inner), F32),
                        pltpu.VMEM((D_STATE, d_inner), F32)],
        compiler_params=pltpu.CompilerParams(
            dimension_semantics=("arbitrary", "arbitrary"), vmem_limit_bytes=VMEM_LIMIT),
        name="prompt_mixer",
    )(x, mod_p, *consts)


def _prompt_mlp_kernel(x_ref, mod_ref, n2g_ref, nfg_ref, w1_ref, w2_ref, o_ref):
    d = x_ref.shape[1]
    sh2 = mod_ref[0, :, 3 * d:4 * d]
    sc2 = mod_ref[0, :, 4 * d:5 * d]
    g2 = mod_ref[0, :, 5 * d:6 * d]
    x2 = _mlp(x_ref[...], sh2, sc2, g2, n2g_ref[...], w1_ref, w2_ref)
    o_ref[...] = _rms(x2, nfg_ref[...])


def _prompt_mlp(x1, mod_p, n2g, nfg, w1, w2, *, tm, rows_per_seq):
    t, d = x1.shape
    per = rows_per_seq // tm
    return pl.pallas_call(
        _prompt_mlp_kernel,
        grid=(t // tm,),
        in_specs=[pl.BlockSpec((tm, d), lambda i: (i, 0)),
                  pl.BlockSpec((1, 1, mod_p.shape[2]), lambda i: (i // per, 0, 0)),
                  _full_spec(n2g.shape), _full_spec(nfg.shape),
                  _full_spec(w1.shape), _full_spec(w2.shape)],
        out_specs=pl.BlockSpec((tm, d), lambda i: (i, 0)),
        out_shape=jax.ShapeDtypeStruct((t, d), F32),
        compiler_params=pltpu.CompilerParams(
            dimension_semantics=("arbitrary",), vmem_limit_bytes=VMEM_LIMIT),
        name="prompt_mlp",
    )(x1, mod_p, n2g, nfg, w1, w2)


def _sample_pre_kernel(x_ref, mod_ref, sa_ref, sbc_ref, n1g_ref, w_abc_ref, w_z_ref, w_xbc_ref,
                       w_dt_ref, w_g_ref, caw_ref, cbw_ref, cbb_ref, dtb_ref, alog_ref, w_ao_ref,
                       e2_ref,
                       ma_ref, na_ref, nbc_ref, dtx_ref, xs_ref, bm_ref, cm_ref, dec_ref, sz_ref,
                       sgb_ref):
    d = x_ref.shape[1]
    d_inner = w_z_ref.shape[1]
    d_xbc = w_xbc_ref.shape[1]
    ka = caw_ref.shape[0]
    kb = cbw_ref.shape[0]
    x = x_ref[...]
    sh1 = mod_ref[:, 0:d]
    sc1 = mod_ref[:, d:2 * d]
    ub = (_rms(x, n1g_ref[...]) * (1.0 + sc1) + sh1).astype(BF16)

    bgate = jnp.dot(ub, w_abc_ref[:, 0:d], preferred_element_type=F32)
    cgate = jnp.dot(ub, w_abc_ref[:, d:2 * d], preferred_element_type=F32)
    hval = jnp.dot(ub, w_abc_ref[:, 2 * d:3 * d], preferred_element_type=F32)
    ch = cgate * hval
    conv = ch * caw_ref[ka - 1:ka, :]
    for k in range(ka - 1):
        conv = conv + sa_ref[:, k * d:(k + 1) * d] * caw_ref[k:k + 1, :]
    na_ref[:, 0:(ka - 2) * d] = sa_ref[:, d:(ka - 1) * d]
    na_ref[:, (ka - 2) * d:(ka - 1) * d] = ch
    y_a = _bdot(bgate * conv, w_ao_ref[...])
    ma_ref[...] = jax.nn.sigmoid(jnp.dot(ub, w_g_ref[:, 0:d], preferred_element_type=F32)) * y_a
    sgb_ref[...] = jax.nn.sigmoid(jnp.dot(ub, w_g_ref[:, d:2 * d], preferred_element_type=F32))

    xbc = jnp.dot(ub, w_xbc_ref[...], preferred_element_type=F32)
    xc = xbc * cbw_ref[kb - 1:kb, :]
    for k in range(kb - 1):
        xc = xc + sbc_ref[:, k * d_xbc:(k + 1) * d_xbc] * cbw_ref[k:k + 1, :]
    xc = _silu(xc + cbb_ref[...])
    nbc_ref[:, 0:(kb - 2) * d_xbc] = sbc_ref[:, d_xbc:(kb - 1) * d_xbc]
    nbc_ref[:, (kb - 2) * d_xbc:(kb - 1) * d_xbc] = xbc

    gn = N_GROUPS * D_STATE
    xs = xc[:, 0:d_inner]
    dt = _softplus(jnp.dot(ub, w_dt_ref[...], preferred_element_type=F32) + dtb_ref[...])
    dec_ref[...] = jnp.exp(dt * (-jnp.exp(alog_ref[...])))
    hi, mid, lo = _split3(dt)
    e3 = e2_ref[0:LANES, :]
    dt_x = (jnp.dot(hi, e3, preferred_element_type=F32) + jnp.dot(mid, e3, preferred_element_type=F32)
            + jnp.dot(lo, e3, preferred_element_type=F32))
    dtx_ref[...] = xs * dt_x
    xs_ref[...] = xs
    bm_ref[...] = xc[:, d_inner:d_inner + gn]
    cm_ref[...] = xc[:, d_inner + gn:d_inner + 2 * gn]
    sz_ref[...] = _silu(jnp.dot(ub, w_z_ref[...], preferred_element_type=F32))


def _sample_pre(x, mod_s, sa, sbc, n1g, w_abc, w_z, w_xbc, w_dt, w_g, caw, cbw, cbb, dtb, alog, w_ao, e2):
    n, d = x.shape
    d_inner = w_z.shape[1]
    gn = N_GROUPS * D_STATE
    shapes = [(n, d), sa.shape, sbc.shape, (n, d_inner), (n, d_inner), (n, gn), (n, gn), (n, LANES),
              (n, d_inner), (n, d)]
    return pl.pallas_call(
        _sample_pre_kernel,
        out_shape=[jax.ShapeDtypeStruct(s, F32) for s in shapes],
        compiler_params=pltpu.CompilerParams(vmem_limit_bytes=VMEM_LIMIT),
        name="sample_pre",
    )(x, mod_s, sa, sbc, n1g, w_abc, w_z, w_xbc, w_dt, w_g, caw, cbw, cbb, dtb, alog, w_ao, e2)


def _sample_state_kernel(dec_ref, h0_ref, dtx_ref, bm_ref, cm_ref, hn_ref, y_ref):
    i = pl.program_id(0)
    n_heads = h0_ref.shape[1]
    p = h0_ref.shape[2]
    per = n_heads // N_GROUPS
    x_hp = dtx_ref[0]
    eye = jnp.where(lax.broadcasted_iota(jnp.int32, (p, p), 0)
                    == lax.broadcasted_iota(jnp.int32, (p, p), 1), 1.0, 0.0).astype(BF16)
    x_t = sum(_dot_nt(eye, part) for part in _split3(x_hp))
    for g in range(N_GROUPS):
        brow = bm_ref[0, g:g + 1, :]
        crow = cm_ref[0, g:g + 1, :]
        hs = []
        for r in range(per):
            h = g * per + r
            hn = dec_ref[i * n_heads + h] * h0_ref[0, h] + x_t[:, h:h + 1] * brow
            hn_ref[0, h] = hn
            hs.append(hn.astype(BF16))
        hg = jnp.concatenate(hs, axis=0)
        cpad = jnp.broadcast_to(crow, (8, crow.shape[1])).astype(BF16)
        y_ref[0, g:g + 1, :] = _dot_nt(cpad, hg)[0:1, :]


def _sample_state(dec_flat, h0, dtx3, bm3, cm3):
    n, n_heads, p, ns = h0.shape
    per = n_heads // N_GROUPS
    return pl.pallas_call(
        _sample_state_kernel,
        grid_spec=pltpu.PrefetchScalarGridSpec(
            num_scalar_prefetch=1,
            grid=(n,),
            in_specs=[pl.BlockSpec((1, n_heads, p, ns), lambda i, s: (i, 0, 0, 0)),
                      pl.BlockSpec((1, n_heads, p), lambda i, s: (i, 0, 0)),
                      pl.BlockSpec((1, N_GROUPS, ns), lambda i, s: (i, 0, 0)),
                      pl.BlockSpec((1, N_GROUPS, ns), lambda i, s: (i, 0, 0))],
            out_specs=[pl.BlockSpec((1, n_heads, p, ns), lambda i, s: (i, 0, 0, 0)),
                       pl.BlockSpec((1, N_GROUPS, per * p), lambda i, s: (i, 0, 0))]),
        out_shape=[jax.ShapeDtypeStruct(h0.shape, F32),
                   jax.ShapeDtypeStruct((n, N_GROUPS, per * p), F32)],
        compiler_params=pltpu.CompilerParams(dimension_semantics=("arbitrary",)),
        name="sample_state",
    )(dec_flat, h0, dtx3, bm3, cm3)


def _sample_post_kernel(x_ref, mod_ref, y_ref, xs_ref, sz_ref, ma_ref, sgb_ref, dexp_ref, sng_ref,
                        w_bo_ref, w_o_ref, n2g_ref, nfg_ref, w1_ref, w2_ref, o_ref):
    d = x_ref.shape[1]
    g1 = mod_ref[:, 2 * d:3 * d]
    sh2 = mod_ref[:, 3 * d:4 * d]
    sc2 = mod_ref[:, 4 * d:5 * d]
    g2 = mod_ref[:, 5 * d:6 * d]
    ys = (y_ref[...] + dexp_ref[...] * xs_ref[...]) * sz_ref[...]
    y_b = _bdot(_group_rmsnorm(ys, sng_ref[...]), w_bo_ref[...])
    merged = ma_ref[...] + sgb_ref[...] * y_b
    x1 = x_ref[...] + g1 * _bdot(merged, w_o_ref[...])
    x2 = _mlp(x1, sh2, sc2, g2, n2g_ref[...], w1_ref, w2_ref)
    o_ref[...] = _rms(x2, nfg_ref[...])


def _sample_post(x, mod_s, y, xs, sz, ma, sgb, dexp, sng, w_bo, w_o, n2g, nfg, w1, w2):
    return pl.pallas_call(
        _sample_post_kernel,
        out_shape=jax.ShapeDtypeStruct(x.shape, F32),
        compiler_params=pltpu.CompilerParams(vmem_limit_bytes=VMEM_LIMIT),
        name="sample_post",
    )(x, mod_s, y, xs, sz, ma, sgb, dexp, sng, w_bo, w_o, n2g, nfg, w1, w2)


def _layer_weights(l, w_ada, b_ada, norm1_g, w_in, conv_a_w, w_a_out, conv_b_w, conv_b_b, dt_bias,
                   a_log, d_skip, ssm_norm_g, w_b_out, w_o, norm2_g, w_mlp1, w_mlp2):
    d = w_in.shape[1]
    d_inner = w_b_out.shape[1]
    d_xbc = conv_b_w.shape[2]
    n_heads = dt_bias.shape[1]
    o_z = 3 * d
    o_xbc = o_z + d_inner
    o_dt = o_xbc + d_xbc
    o_g = o_dt + n_heads
    wi = w_in[l]
    row = lambda v: v.reshape(1, -1)
    lane_pad = lambda v: jnp.pad(v, ((0, 0), (0, LANES - v.shape[1])))
    return dict(
        w_ada=w_ada[l].astype(BF16), b_ada=row(b_ada[l]), n1g=row(norm1_g[l]),
        w_abc=wi[:, 0:o_z].astype(BF16), w_z=wi[:, o_z:o_xbc].astype(BF16),
        w_xbc=wi[:, o_xbc:o_dt].astype(BF16), w_dt=lane_pad(wi[:, o_dt:o_g]).astype(BF16),
        w_g=wi[:, o_g:o_g + 2 * d].astype(BF16),
        caw=conv_a_w[l], cbw=conv_b_w[l], cbb=row(conv_b_b[l]),
        dtb=lane_pad(row(dt_bias[l])), alog=lane_pad(row(a_log[l])),
        dexp=row(jnp.repeat(d_skip[l], HEAD_DIM)), sng=row(ssm_norm_g[l]),
        w_ao=w_a_out[l].astype(BF16), w_bo=w_b_out[l].astype(BF16), w_o=w_o[l].astype(BF16),
        n2g=row(norm2_g[l]), w1=w_mlp1[l].astype(BF16), w2=w_mlp2[l].astype(BF16))


def kernel(x_prompt, x_sample, c_prompt, c_sample, state_shortconv, state_ssm_conv, state_ssm, w_ada, b_ada, norm1_g, w_in, conv_a_w, w_a_out, conv_b_w, conv_b_b, dt_bias, a_log, d_skip, ssm_norm_g, w_b_out, w_o, norm2_g, w_mlp1, w_mlp2, norm_f_g):
    bp, lp, d = x_prompt.shape
    bs, ls, _ = x_sample.shape
    assert ls == 1
    depth = w_in.shape[0]
    d_inner = w_b_out.shape[1]
    n_heads = dt_bias.shape[1]
    assert n_heads <= LANES and d_inner == n_heads * HEAD_DIM

    e1 = (jnp.arange(LANES)[:, None] == (jnp.arange(d_inner)[None, :] // HEAD_DIM)).astype(BF16)
    e2 = jnp.concatenate([e1, e1], axis=0)
    nfg = norm_f_g.reshape(1, -1)

    assert depth == 1
    w = _layer_weights(0, w_ada, b_ada, norm1_g, w_in, conv_a_w, w_a_out, conv_b_w, conv_b_b,
                       dt_bias, a_log, d_skip, ssm_norm_g, w_b_out, w_o, norm2_g, w_mlp1, w_mlp2)
    mod = _modulation(jnp.concatenate([c_prompt, c_sample], axis=0), w["w_ada"], w["b_ada"])
    mod_p = mod[:bp].reshape(bp, 1, -1)
    mod_s = mod[bp:]

    x1, na, nbc, nssm = _prompt_mixer(
        x_prompt, mod_p, w["n1g"], w["w_abc"], w["w_z"], w["w_xbc"], w["w_dt"], w["w_g"], w["caw"],
        w["cbw"], w["cbb"], w["dtb"], w["alog"], w["dexp"], w["sng"], w["w_ao"], w["w_bo"],
        w["w_o"], e2, tq=256)
    yp = _prompt_mlp(x1.reshape(bp * lp, d), mod_p, w["n2g"], nfg, w["w1"], w["w2"],
                     tm=512, rows_per_seq=lp).reshape(bp, lp, d)

    xs0 = x_sample.reshape(bs, d)
    sa = state_shortconv[0].reshape(bs, -1)
    sbc = state_ssm_conv[0].reshape(bs, -1)
    ma, na_s, nbc_s, dtx, xs, bm, cm, dec, sz, sgb = _sample_pre(
        xs0, mod_s, sa, sbc, w["n1g"], w["w_abc"], w["w_z"], w["w_xbc"], w["w_dt"], w["w_g"],
        w["caw"], w["cbw"], w["cbb"], w["dtb"], w["alog"], w["w_ao"], e2)
    hn, y3 = _sample_state(dec[:, :n_heads].reshape(-1), state_ssm[0],
                           dtx.reshape(bs, n_heads, HEAD_DIM),
                           bm.reshape(bs, N_GROUPS, D_STATE), cm.reshape(bs, N_GROUPS, D_STATE))
    ys = _sample_post(xs0, mod_s, y3.reshape(bs, d_inner), xs, sz, ma, sgb, w["dexp"], w["sng"],
                      w["w_bo"], w["w_o"], w["n2g"], nfg, w["w1"], w["w2"])

    return (yp, ys.reshape(bs, ls, d), na[None], nbc[None], nssm[None],
            na_s.reshape(state_shortconv.shape), nbc_s.reshape(state_ssm_conv.shape), hn[None])
```

```python
import functools

import jax
import jax.numpy as jnp
from jax import lax
from jax.experimental import pallas as pl
from jax.experimental.pallas import tpu as pltpu

F32 = jnp.float32
BF16 = jnp.bfloat16

EPS = 1e-6
HEAD_DIM = 64
N_GROUPS = 4
D_STATE = 128
LANES = 128
SSD_CHUNK = 128
QUAD = 4
VMEM_LIMIT = 56 * 1024 * 1024


def _full_spec(shape):
    nd = len(shape)
    return pl.BlockSpec(shape, lambda *_: (0,) * nd, pipeline_mode=pl.Buffered(1))


def _bdot(a, b):
    return jnp.dot(a.astype(BF16), b, preferred_element_type=F32)


def _dot_nt(a, b):
    return lax.dot_general(a, b, (((1,), (1,)), ((), ())), preferred_element_type=F32)


def _rms(x, g):
    ms = jnp.mean(x * x, axis=-1, keepdims=True)
    return x * lax.rsqrt(ms + EPS) * g


def _silu(x):
    return x * jax.nn.sigmoid(x)


def _softplus(x):
    return jnp.maximum(x, 0.0) + jnp.log1p(jnp.exp(-jnp.abs(x)))


def _split2(v):
    hi = v.astype(BF16)
    lo = (v - hi.astype(F32)).astype(BF16)
    return hi, lo


def _split3(v):
    hi = v.astype(BF16)
    r = v - hi.astype(F32)
    mid = r.astype(BF16)
    lo = (r - mid.astype(F32)).astype(BF16)
    return hi, mid, lo


def _expand_heads(v, e2_ref):
    hi, lo = _split2(v)
    return jnp.dot(jnp.concatenate([hi, lo], axis=1), e2_ref[...], preferred_element_type=F32)


def _group_rmsnorm(y, g):
    d = y.shape[-1]
    w = d // N_GROUPS
    parts = []
    for k in range(N_GROUPS):
        yk = y[:, k * w:(k + 1) * w]
        ms = jnp.mean(yk * yk, axis=-1, keepdims=True)
        parts.append(yk * lax.rsqrt(ms + EPS))
    return jnp.concatenate(parts, axis=1) * g


def _mlp(x, sh2, sc2, g2, n2g, w1_ref, w2_ref):
    d_ff = w1_ref.shape[1]
    blk = 1024
    ub = (_rms(x, n2g) * (1.0 + sc2) + sh2).astype(BF16)
    acc = None
    for j in range(d_ff // blk):
        h = jnp.dot(ub, w1_ref[:, j * blk:(j + 1) * blk], preferred_element_type=F32)
        h = jnp.square(jnp.maximum(h, 0.0))
        p = _bdot(h, w2_ref[j * blk:(j + 1) * blk, :])
        acc = p if acc is None else acc + p
    return x + g2 * acc


def _mod_kernel(c_ref, w_ref, b_ref, o_ref):
    c = c_ref[...]
    o_ref[...] = _bdot(_silu(c), w_ref[...]) + b_ref[...]


def _modulation(c_all, w_ada, b_ada):
    n, d = c_all.shape
    dm = w_ada.shape[1]
    blk = 1024
    return pl.pallas_call(
        _mod_kernel,
        grid=(dm // blk,),
        in_specs=[pl.BlockSpec((n, d), lambda j: (0, 0)),
                  pl.BlockSpec((d, blk), lambda j: (0, j)),
                  pl.BlockSpec((1, blk), lambda j: (0, j))],
        out_specs=pl.BlockSpec((n, blk), lambda j: (0, j)),
        out_shape=jax.ShapeDtypeStruct((n, dm), F32),
        name="modulation",
    )(c_all, w_ada, b_ada)


def _tile_causal_conv(buf, cur, w_ref, pad):
    k_w = w_ref.shape[0]
    tq = cur.shape[0]
    buf[pad:pad + tq, :] = cur
    out = cur * w_ref[k_w - 1:k_w, :]
    for k in range(k_w - 1):
        lo = pad - (k_w - 1) + k
        out = out + buf[lo:lo + tq, :] * w_ref[k:k + 1, :]
    last = buf[pad + tq - (k_w - 1):pad + tq, :]
    buf[pad - (k_w - 1):pad, :] = last
    return out, last


def _ssd_chunk(xs_c, b_c, c_c, dt_c, a_row, d_row, ht_ref, e2_ref):
    q = xs_c.shape[0]
    n = D_STATE
    gw = xs_c.shape[1] // N_GROUPS
    qw = QUAD * HEAD_DIM
    row = lax.broadcasted_iota(jnp.int32, (q, q), 0)
    col = lax.broadcasted_iota(jnp.int32, (q, q), 1)
    causal = row >= col
    tri = jnp.where(causal, 1.0, 0.0).astype(BF16)
    lane_head = lax.broadcasted_iota(jnp.int32, (q, qw), 1) // HEAD_DIM
    head_mask = [jnp.where(lane_head == r, 1.0, 0.0).astype(BF16) for r in range(QUAD)]

    da = dt_c * a_row
    acs = sum(jnp.dot(tri, p, preferred_element_type=F32) for p in _split3(da))
    acs_t = acs.T
    dt_t = dt_c.T
    ea = jnp.exp(acs)
    dd = jnp.exp(acs[q - 1:q, :] - acs) * dt_c
    ea_x = _expand_heads(ea, e2_ref)
    dd_x = _expand_heads(dd, e2_ref)
    xd = (xs_c * dd_x).astype(BF16)
    bb = b_c.astype(BF16)
    cb_ = c_c.astype(BF16)

    y_parts = []
    for g in range(N_GROUPS):
        bg = bb[:, g * n:(g + 1) * n]
        cg = cb_[:, g * n:(g + 1) * n]
        cbm = _dot_nt(cg, bg)
        ht_g = ht_ref[:, g * gw:(g + 1) * gw]
        y_off = jnp.dot(cg, ht_g.astype(BF16), preferred_element_type=F32)
        for qd in range(gw // qw):
            lo = g * gw + qd * qw
            h0 = lo // HEAD_DIM
            ws = []
            for r in range(QUAD):
                h = h0 + r
                seg = acs[:, h:h + 1] - acs_t[h:h + 1, :]
                lm = jnp.exp(jnp.where(causal, seg, -jnp.inf))
                ws.append((cbm * lm * dt_t[h:h + 1, :]).astype(BF16))
            wcat = jnp.concatenate(ws, axis=1)
            xq = xs_c[:, lo:lo + qw]
            xq_b = xq.astype(BF16)
            rhs = jnp.concatenate([xq_b * head_mask[r] for r in range(QUAD)], axis=0)
            y_diag = jnp.dot(wcat, rhs, preferred_element_type=F32)
            y_parts.append(y_diag + y_off[:, qd * qw:(qd + 1) * qw] * ea_x[:, lo:lo + qw]
                           + d_row[:, lo:lo + qw] * xq)
        bg_t = b_c[:, g * n:(g + 1) * n].T.astype(BF16)
        st = jnp.dot(bg_t, xd[:, g * gw:(g + 1) * gw], preferred_element_type=F32)
        ht_ref[:, g * gw:(g + 1) * gw] = ht_g * ea_x[q - 1:q, g * gw:(g + 1) * gw] + st
    return jnp.concatenate(y_parts, axis=1)


def _prompt_mixer_kernel(x_ref, mod_ref, n1g_ref, w_abc_ref, w_z_ref, w_xbc_ref, w_dt_ref, w_g_ref,
                         caw_ref, cbw_ref, cbb_ref, dtb_ref, alog_ref, dexp_ref, sng_ref,
                         w_ao_ref, w_bo_ref, w_o_ref, e2_ref,
                         x1_ref, na_ref, nbc_ref, nssm_ref,
                         cbuf, xbuf, ybuf, ht_ref):
    t = pl.program_id(1)
    nt = pl.num_programs(1)
    tq = x_ref.shape[1]
    d = x_ref.shape[2]
    d_inner = w_z_ref.shape[1]
    ka = caw_ref.shape[0]
    kb = cbw_ref.shape[0]
    pad = 8

    @pl.when(t == 0)
    def _():
        cbuf[0:pad, :] = jnp.zeros((pad, cbuf.shape[1]), F32)
        xbuf[0:pad, :] = jnp.zeros((pad, xbuf.shape[1]), F32)
        ht_ref[...] = jnp.zeros(ht_ref.shape, F32)

    x = x_ref[0]
    sh1 = mod_ref[0, :, 0:d]
    sc1 = mod_ref[0, :, d:2 * d]
    g1 = mod_ref[0, :, 2 * d:3 * d]
    ub = (_rms(x, n1g_ref[...]) * (1.0 + sc1) + sh1).astype(BF16)

    def proj(w_ref, lo=None, hi=None):
        w = w_ref[...] if lo is None else w_ref[:, lo:hi]
        return jnp.dot(ub, w, preferred_element_type=F32)

    xbc = proj(w_xbc_ref)
    dt = _softplus(proj(w_dt_ref) + dtb_ref[...])
    a_row = -jnp.exp(alog_ref[...])
    xc, new_bc = _tile_causal_conv(xbuf, xbc, cbw_ref, pad)
    cgate = proj(w_abc_ref, d, 2 * d)
    hval = proj(w_abc_ref, 2 * d, 3 * d)
    xc = _silu(xc + cbb_ref[...])
    bgate = proj(w_abc_ref, 0, d)
    conv, new_a = _tile_causal_conv(cbuf, cgate * hval, caw_ref, pad)

    gn = N_GROUPS * D_STATE
    fillers = [lambda: proj(w_z_ref), lambda: _bdot(bgate * conv, w_ao_ref[...])]
    filled = []
    for c in range(tq // SSD_CHUNK):
        s = slice(c * SSD_CHUNK, (c + 1) * SSD_CHUNK)
        if c < len(fillers):
            filled.append(fillers[c]())
        ybuf[s, :] = _ssd_chunk(xc[s, 0:d_inner], xc[s, d_inner:d_inner + gn],
                                xc[s, d_inner + gn:d_inner + 2 * gn], dt[s, :], a_row,
                                dexp_ref[...], ht_ref, e2_ref)
    filled += [f() for f in fillers[len(filled):]]
    z, y_a = filled

    merged = jax.nn.sigmoid(proj(w_g_ref, 0, d)) * y_a
    y_b = _bdot(_group_rmsnorm(ybuf[...] * _silu(z), sng_ref[...]), w_bo_ref[...])
    merged = merged + jax.nn.sigmoid(proj(w_g_ref, d, 2 * d)) * y_b
    x1_ref[0] = x + g1 * _bdot(merged, w_o_ref[...])

    @pl.when(t == nt - 1)
    def _():
        na_ref[0] = new_a
        nbc_ref[0] = new_bc
        h = ht_ref[...].T
        nssm_ref[0] = h.reshape(nssm_ref.shape[1:])


def _prompt_mixer(x, mod_p, n1g, w_abc, w_z, w_xbc, w_dt, w_g, caw, cbw, cbb, dtb, alog, dexp, sng,
                  w_ao, w_bo, w_o, e2, *, tq):
    b, l, d = x.shape
    d_inner = w_z.shape[1]
    d_xbc = w_xbc.shape[1]
    n_heads = d_inner // HEAD_DIM
    consts = (n1g, w_abc, w_z, w_xbc, w_dt, w_g, caw, cbw, cbb, dtb, alog, dexp, sng, w_ao, w_bo, w_o, e2)
    return pl.pallas_call(
        _prompt_mixer_kernel,
        grid=(b, l // tq),
        in_specs=[pl.BlockSpec((1, tq, d), lambda i, t: (i, t, 0)),
                  pl.BlockSpec((1, 1, mod_p.shape[2]), lambda i, t: (i, 0, 0))]
                 + [_full_spec(c.shape) for c in consts],
        out_specs=[pl.BlockSpec((1, tq, d), lambda i, t: (i, t, 0)),
                   pl.BlockSpec((1, caw.shape[0] - 1, d), lambda i, t: (i, 0, 0)),
                   pl.BlockSpec((1, cbw.shape[0] - 1, d_xbc), lambda i, t: (i, 0, 0)),
                   pl.BlockSpec((1, n_heads, HEAD_DIM, D_STATE), lambda i, t: (i, 0, 0, 0))],
        out_shape=[jax.ShapeDtypeStruct((b, l, d), F32),
                   jax.ShapeDtypeStruct((b, caw.shape[0] - 1, d), F32),
                   jax.ShapeDtypeStruct((b, cbw.shape[0] - 1, d_xbc), F32),
                   jax.ShapeDtypeStruct((b, n_heads, HEAD_DIM, D_STATE), F32)],
        scratch_shapes=[pltpu.VMEM((tq + 8, d), F32),
                        pltpu.VMEM((tq + 8, d_xbc), F32),
                        pltpu.VMEM((tq, d_inner), F32),
                        pltpu.VMEM((D_STATE, d_inner), F32)],
        compiler_params=pltpu.CompilerParams(
            dimension_semantics=("arbitrary", "arbitrary"), vmem_limit_bytes=VMEM_LIMIT),
        name="prompt_mixer",
    )(x, mod_p, *consts)


def _prompt_mlp_kernel(x_ref, mod_ref, n2g_ref, nfg_ref, w1_ref, w2_ref, o_ref):
    d = x_ref.shape[1]
    sh2 = mod_ref[0, :, 3 * d:4 * d]
    sc2 = mod_ref[0, :, 4 * d:5 * d]
    g2 = mod_ref[0, :, 5 * d:6 * d]
    x2 = _mlp(x_ref[...], sh2, sc2, g2, n2g_ref[...], w1_ref, w2_ref)
    o_ref[...] = _rms(x2, nfg_ref[...])


def _prompt_mlp(x1, mod_p, n2g, nfg, w1, w2, *, tm, rows_per_seq):
    t, d = x1.shape
    per = rows_per_seq // tm
    return pl.pallas_call(
        _prompt_mlp_kernel,
        grid=(t // tm,),
        in_specs=[pl.BlockSpec((tm, d), lambda i: (i, 0)),
                  pl.BlockSpec((1, 1, mod_p.shape[2]), lambda i: (i // per, 0, 0)),
                  _full_spec(n2g.shape), _full_spec(nfg.shape),
                  _full_spec(w1.shape), _full_spec(w2.shape)],
        out_specs=pl.BlockSpec((tm, d), lambda i: (i, 0)),
        out_shape=jax.ShapeDtypeStruct((t, d), F32),
        compiler_params=pltpu.CompilerParams(
            dimension_semantics=("arbitrary",), vmem_limit_bytes=VMEM_LIMIT),
        name="prompt_mlp",
    )(x1, mod_p, n2g, nfg, w1, w2)


def _sample_pre_kernel(x_ref, mod_ref, sa_ref, sbc_ref, n1g_ref, w_abc_ref, w_z_ref, w_xbc_ref,
                       w_dt_ref, w_g_ref, caw_ref, cbw_ref, cbb_ref, dtb_ref, alog_ref, w_ao_ref,
                       e2_ref,
                       ma_ref, na_ref, nbc_ref, dtx_ref, xs_ref, bm_ref, cm_ref, dec_ref, sz_ref,
                       sgb_ref):
    d = x_ref.shape[1]
    d_inner = w_z_ref.shape[1]
    d_xbc = w_xbc_ref.shape[1]
    ka = caw_ref.shape[0]
    kb = cbw_ref.shape[0]
    x = x_ref[...]
    sh1 = mod_ref[:, 0:d]
    sc1 = mod_ref[:, d:2 * d]
    ub = (_rms(x, n1g_ref[...]) * (1.0 + sc1) + sh1).astype(BF16)

    bgate = jnp.dot(ub, w_abc_ref[:, 0:d], preferred_element_type=F32)
    cgate = jnp.dot(ub, w_abc_ref[:, d:2 * d], preferred_element_type=F32)
    hval = jnp.dot(ub, w_abc_ref[:, 2 * d:3 * d], preferred_element_type=F32)
    ch = cgate * hval
    conv = ch * caw_ref[ka - 1:ka, :]
    for k in range(ka - 1):
        conv = conv + sa_ref[:, k * d:(k + 1) * d] * caw_ref[k:k + 1, :]
    na_ref[:, 0:(ka - 2) * d] = sa_ref[:, d:(ka - 1) * d]
    na_ref[:, (ka - 2) * d:(ka - 1) * d] = ch
    y_a = _bdot(bgate * conv, w_ao_ref[...])
    ma_ref[...] = jax.nn.sigmoid(jnp.dot(ub, w_g_ref[:, 0:d], preferred_element_type=F32)) * y_a
    sgb_ref[...] = jax.nn.sigmoid(jnp.dot(ub, w_g_ref[:, d:2 * d], preferred_element_type=F32))

    xbc = jnp.dot(ub, w_xbc_ref[...], preferred_element_type=F32)
    xc = xbc * cbw_ref[kb - 1:kb, :]
    for k in range(kb - 1):
        xc = xc + sbc_ref[:, k * d_xbc:(k + 1) * d_xbc] * cbw_ref[k:k + 1, :]
    xc = _silu(xc + cbb_ref[...])
    nbc_ref[:, 0:(kb - 2) * d_xbc] = sbc_ref[:, d_xbc:(kb - 1) * d_xbc]
    nbc_ref[:, (kb - 2) * d_xbc:(kb - 1) * d_xbc] = xbc

    gn = N_GROUPS * D_STATE
    xs = xc[:, 0:d_inner]
    dt = _softplus(jnp.dot(ub, w_dt_ref[...], preferred_element_type=F32) + dtb_ref[...])
    dec_ref[...] = jnp.exp(dt * (-jnp.exp(alog_ref[...])))
    hi, mid, lo = _split3(dt)
    e3 = e2_ref[0:LANES, :]
    dt_x = (jnp.dot(hi, e3, preferred_element_type=F32) + jnp.dot(mid, e3, preferred_element_type=F32)
            + jnp.dot(lo, e3, preferred_element_type=F32))
    dtx_ref[...] = xs * dt_x
    xs_ref[...] = xs
    bm_ref[...] = xc[:, d_inner:d_inner + gn]
    cm_ref[...] = xc[:, d_inner + gn:d_inner + 2 * gn]
    sz_ref[...] = _silu(jnp.dot(ub, w_z_ref[...], preferred_element_type=F32))


def _sample_pre(x, mod_s, sa, sbc, n1g, w_abc, w_z, w_xbc, w_dt, w_g, caw, cbw, cbb, dtb, alog, w_ao, e2):
    n, d = x.shape
    d_inner = w_z.shape[1]
    gn = N_GROUPS * D_STATE
    shapes = [(n, d), sa.shape, sbc.shape, (n, d_inner), (n, d_inner), (n, gn), (n, gn), (n, LANES),
              (n, d_inner), (n, d)]
    return pl.pallas_call(
        _sample_pre_kernel,
        out_shape=[jax.ShapeDtypeStruct(s, F32) for s in shapes],
        compiler_params=pltpu.CompilerParams(vmem_limit_bytes=VMEM_LIMIT),
        name="sample_pre",
    )(x, mod_s, sa, sbc, n1g, w_abc, w_z, w_xbc, w_dt, w_g, caw, cbw, cbb, dtb, alog, w_ao, e2)


def _sample_state_kernel(dec_ref, h0_ref, dtx_ref, bm_ref, cm_ref, hn_ref, y_ref):
    i = pl.program_id(0)
    bb, n_heads, p, _ = h0_ref.shape
    per = n_heads // N_GROUPS
    eye = jnp.where(lax.broadcasted_iota(jnp.int32, (p, p), 0)
                    == lax.broadcasted_iota(jnp.int32, (p, p), 1), 1.0, 0.0).astype(BF16)
    for j in range(bb):
        x_hp = dtx_ref[j]
        x_t = sum(_dot_nt(eye, part) for part in _split3(x_hp))
        for g in range(N_GROUPS):
            brow = bm_ref[j, g:g + 1, :]
            crow = cm_ref[j, g:g + 1, :]
            hs = []
            for r in range(per):
                h = g * per + r
                hn = dec_ref[(i * bb + j) * n_heads + h] * h0_ref[j, h] + x_t[:, h:h + 1] * brow
                hn_ref[j, h] = hn
                hs.append(hn.astype(BF16))
            hg = jnp.concatenate(hs, axis=0)
            cpad = jnp.broadcast_to(crow, (8, crow.shape[1])).astype(BF16)
            y_ref[j, g:g + 1, :] = _dot_nt(cpad, hg)[0:1, :]


def _sample_state(dec_flat, h0, dtx3, bm3, cm3, *, bb):
    n, n_heads, p, ns = h0.shape
    per = n_heads // N_GROUPS
    return pl.pallas_call(
        _sample_state_kernel,
        grid_spec=pltpu.PrefetchScalarGridSpec(
            num_scalar_prefetch=1,
            grid=(n // bb,),
            in_specs=[pl.BlockSpec((bb, n_heads, p, ns), lambda i, s: (i, 0, 0, 0)),
                      pl.BlockSpec((bb, n_heads, p), lambda i, s: (i, 0, 0)),
                      pl.BlockSpec((bb, N_GROUPS, ns), lambda i, s: (i, 0, 0)),
                      pl.BlockSpec((bb, N_GROUPS, ns), lambda i, s: (i, 0, 0))],
            out_specs=[pl.BlockSpec((bb, n_heads, p, ns), lambda i, s: (i, 0, 0, 0)),
                       pl.BlockSpec((bb, N_GROUPS, per * p), lambda i, s: (i, 0, 0))]),
        out_shape=[jax.ShapeDtypeStruct(h0.shape, F32),
                   jax.ShapeDtypeStruct((n, N_GROUPS, per * p), F32)],
        compiler_params=pltpu.CompilerParams(
            dimension_semantics=("arbitrary",), vmem_limit_bytes=VMEM_LIMIT),
        name="sample_state",
    )(dec_flat, h0, dtx3, bm3, cm3)


def _sample_post_kernel(x_ref, mod_ref, y_ref, xs_ref, sz_ref, ma_ref, sgb_ref, dexp_ref, sng_ref,
                        w_bo_ref, w_o_ref, n2g_ref, nfg_ref, w1_ref, w2_ref, o_ref):
    d = x_ref.shape[1]
    g1 = mod_ref[:, 2 * d:3 * d]
    sh2 = mod_ref[:, 3 * d:4 * d]
    sc2 = mod_ref[:, 4 * d:5 * d]
    g2 = mod_ref[:, 5 * d:6 * d]
    ys = (y_ref[...] + dexp_ref[...] * xs_ref[...]) * sz_ref[...]
    y_b = _bdot(_group_rmsnorm(ys, sng_ref[...]), w_bo_ref[...])
    merged = ma_ref[...] + sgb_ref[...] * y_b
    x1 = x_ref[...] + g1 * _bdot(merged, w_o_ref[...])
    x2 = _mlp(x1, sh2, sc2, g2, n2g_ref[...], w1_ref, w2_ref)
    o_ref[...] = _rms(x2, nfg_ref[...])


def _sample_post(x, mod_s, y, xs, sz, ma, sgb, dexp, sng, w_bo, w_o, n2g, nfg, w1, w2):
    return pl.pallas_call(
        _sample_post_kernel,
        out_shape=jax.ShapeDtypeStruct(x.shape, F32),
        compiler_params=pltpu.CompilerParams(vmem_limit_bytes=VMEM_LIMIT),
        name="sample_post",
    )(x, mod_s, y, xs, sz, ma, sgb, dexp, sng, w_bo, w_o, n2g, nfg, w1, w2)


def _layer_weights(l, w_ada, b_ada, norm1_g, w_in, conv_a_w, w_a_out, conv_b_w, conv_b_b, dt_bias,
                   a_log, d_skip, ssm_norm_g, w_b_out, w_o, norm2_g, w_mlp1, w_mlp2):
    d = w_in.shape[1]
    d_inner = w_b_out.shape[1]
    d_xbc = conv_b_w.shape[2]
    n_heads = dt_bias.shape[1]
    o_z = 3 * d
    o_xbc = o_z + d_inner
    o_dt = o_xbc + d_xbc
    o_g = o_dt + n_heads
    wi = w_in[l]
    row = lambda v: v.reshape(1, -1)
    lane_pad = lambda v: jnp.pad(v, ((0, 0), (0, LANES - v.shape[1])))
    return dict(
        w_ada=w_ada[l].astype(BF16), b_ada=row(b_ada[l]), n1g=row(norm1_g[l]),
        w_abc=wi[:, 0:o_z].astype(BF16), w_z=wi[:, o_z:o_xbc].astype(BF16),
        w_xbc=wi[:, o_xbc:o_dt].astype(BF16), w_dt=lane_pad(wi[:, o_dt:o_g]).astype(BF16),
        w_g=wi[:, o_g:o_g + 2 * d].astype(BF16),
        caw=conv_a_w[l], cbw=conv_b_w[l], cbb=row(conv_b_b[l]),
        dtb=lane_pad(row(dt_bias[l])), alog=lane_pad(row(a_log[l])),
        dexp=row(jnp.repeat(d_skip[l], HEAD_DIM)), sng=row(ssm_norm_g[l]),
        w_ao=w_a_out[l].astype(BF16), w_bo=w_b_out[l].astype(BF16), w_o=w_o[l].astype(BF16),
        n2g=row(norm2_g[l]), w1=w_mlp1[l].astype(BF16), w2=w_mlp2[l].astype(BF16))


def kernel(x_prompt, x_sample, c_prompt, c_sample, state_shortconv, state_ssm_conv, state_ssm, w_ada, b_ada, norm1_g, w_in, conv_a_w, w_a_out, conv_b_w, conv_b_b, dt_bias, a_log, d_skip, ssm_norm_g, w_b_out, w_o, norm2_g, w_mlp1, w_mlp2, norm_f_g):
    bp, lp, d = x_prompt.shape
    bs, ls, _ = x_sample.shape
    assert ls == 1
    depth = w_in.shape[0]
    d_inner = w_b_out.shape[1]
    n_heads = dt_bias.shape[1]
    assert n_heads <= LANES and d_inner == n_heads * HEAD_DIM

    e1 = (jnp.arange(LANES)[:, None] == (jnp.arange(d_inner)[None, :] // HEAD_DIM)).astype(BF16)
    e2 = jnp.concatenate([e1, e1], axis=0)
    nfg = norm_f_g.reshape(1, -1)

    assert depth == 1
    w = _layer_weights(0, w_ada, b_ada, norm1_g, w_in, conv_a_w, w_a_out, conv_b_w, conv_b_b,
                       dt_bias, a_log, d_skip, ssm_norm_g, w_b_out, w_o, norm2_g, w_mlp1, w_mlp2)
    mod = _modulation(jnp.concatenate([c_prompt, c_sample], axis=0), w["w_ada"], w["b_ada"])
    mod_p = mod[:bp].reshape(bp, 1, -1)
    mod_s = mod[bp:]

    x1, na, nbc, nssm = _prompt_mixer(
        x_prompt, mod_p, w["n1g"], w["w_abc"], w["w_z"], w["w_xbc"], w["w_dt"], w["w_g"], w["caw"],
        w["cbw"], w["cbb"], w["dtb"], w["alog"], w["dexp"], w["sng"], w["w_ao"], w["w_bo"],
        w["w_o"], e2, tq=256)
    yp = _prompt_mlp(x1.reshape(bp * lp, d), mod_p, w["n2g"], nfg, w["w1"], w["w2"],
                     tm=512, rows_per_seq=lp).reshape(bp, lp, d)

    xs0 = x_sample.reshape(bs, d)
    sa = state_shortconv[0].reshape(bs, -1)
    sbc = state_ssm_conv[0].reshape(bs, -1)
    ma, na_s, nbc_s, dtx, xs, bm, cm, dec, sz, sgb = _sample_pre(
        xs0, mod_s, sa, sbc, w["n1g"], w["w_abc"], w["w_z"], w["w_xbc"], w["w_dt"], w["w_g"],
        w["caw"], w["cbw"], w["cbb"], w["dtb"], w["alog"], w["w_ao"], e2)
    hn, y3 = _sample_state(dec[:, :n_heads].reshape(-1), state_ssm[0],
                           dtx.reshape(bs, n_heads, HEAD_DIM),
                           bm.reshape(bs, N_GROUPS, D_STATE), cm.reshape(bs, N_GROUPS, D_STATE), bb=4)
    ys = _sample_post(xs0, mod_s, y3.reshape(bs, d_inner), xs, sz, ma, sgb, w["dexp"], w["sng"],
                      w["w_bo"], w["w_o"], w["n2g"], nfg, w["w1"], w["w2"])

    return (yp, ys.reshape(bs, ls, d), na[None], nbc[None], nssm[None],
            na_s.reshape(state_shortconv.shape), nbc_s.reshape(state_ssm_conv.shape), hn[None])
```

```python
import jax
import jax.numpy as jnp
from jax import lax
from jax.experimental import pallas as pl
from jax.experimental.pallas import tpu as pltpu

F32 = jnp.float32
BF16 = jnp.bfloat16

EPS = 1e-6
HEAD_DIM = 64
N_GROUPS = 4
D_STATE = 128
LANES = 128
SSD_CHUNK = 128
QUAD = 4
MLP_BLOCK = 1024
VMEM_LIMIT = 56 * 1024 * 1024


def _full_spec(shape):
    nd = len(shape)
    return pl.BlockSpec(shape, lambda *_: (0,) * nd, pipeline_mode=pl.Buffered(1))


def _bdot(a, b):
    return jnp.dot(a.astype(BF16), b, preferred_element_type=F32)


def _dot_nt(a, b):
    return lax.dot_general(a, b, (((1,), (1,)), ((), ())), preferred_element_type=F32)


def _rms(x, g):
    ms = jnp.mean(x * x, axis=-1, keepdims=True)
    return x * lax.rsqrt(ms + EPS) * g


def _silu(x):
    return x * jax.nn.sigmoid(x)


def _softplus(x):
    return jnp.maximum(x, 0.0) + jnp.log1p(jnp.exp(-jnp.abs(x)))


def _split2(v):
    hi = v.astype(BF16)
    lo = (v - hi.astype(F32)).astype(BF16)
    return hi, lo


def _split3(v):
    hi = v.astype(BF16)
    r = v - hi.astype(F32)
    mid = r.astype(BF16)
    lo = (r - mid.astype(F32)).astype(BF16)
    return hi, mid, lo


def _expand_heads(v, e2_ref):
    hi, lo = _split2(v)
    return jnp.dot(jnp.concatenate([hi, lo], axis=1), e2_ref[...], preferred_element_type=F32)


def _group_rmsnorm(y, g):
    d = y.shape[-1]
    w = d // N_GROUPS
    parts = []
    for k in range(N_GROUPS):
        yk = y[:, k * w:(k + 1) * w]
        ms = jnp.mean(yk * yk, axis=-1, keepdims=True)
        parts.append(yk * lax.rsqrt(ms + EPS))
    return jnp.concatenate(parts, axis=1) * g


def _mlp(x, sh2, sc2, g2, n2g, w1_ref, w2_ref, between=None):
    d_ff = w1_ref.shape[1]
    blk = MLP_BLOCK
    ub = (_rms(x, n2g) * (1.0 + sc2) + sh2).astype(BF16)
    acc = None
    for j in range(d_ff // blk):
        if between is not None:
            between(j)
        h = jnp.dot(ub, w1_ref[:, j * blk:(j + 1) * blk], preferred_element_type=F32)
        h = jnp.square(jnp.maximum(h, 0.0))
        p = _bdot(h, w2_ref[j * blk:(j + 1) * blk, :])
        acc = p if acc is None else acc + p
    return x + g2 * acc


def _mod_kernel(c_ref, w_ref, b_ref, o_ref):
    c = c_ref[...]
    o_ref[...] = _bdot(_silu(c), w_ref[...]) + b_ref[...]


def _modulation(c_all, w_ada, b_ada):
    n, d = c_all.shape
    dm = w_ada.shape[1]
    blk = 1024
    return pl.pallas_call(
        _mod_kernel,
        grid=(dm // blk,),
        in_specs=[pl.BlockSpec((n, d), lambda j: (0, 0)),
                  pl.BlockSpec((d, blk), lambda j: (0, j)),
                  pl.BlockSpec((1, blk), lambda j: (0, j))],
        out_specs=pl.BlockSpec((n, blk), lambda j: (0, j)),
        out_shape=jax.ShapeDtypeStruct((n, dm), F32),
        name="modulation",
    )(c_all, w_ada, b_ada)


def _tile_causal_conv(buf, cur, w_ref, pad):
    k_w = w_ref.shape[0]
    tq = cur.shape[0]
    buf[pad:pad + tq, :] = cur
    out = cur * w_ref[k_w - 1:k_w, :]
    for k in range(k_w - 1):
        lo = pad - (k_w - 1) + k
        out = out + buf[lo:lo + tq, :] * w_ref[k:k + 1, :]
    last = buf[pad + tq - (k_w - 1):pad + tq, :]
    buf[pad - (k_w - 1):pad, :] = last
    return out, last


def _ssd_chunk(xs_c, b_c, c_c, dt_c, a_row, d_row, ht_ref, e2_ref):
    q = xs_c.shape[0]
    n = D_STATE
    gw = xs_c.shape[1] // N_GROUPS
    qw = QUAD * HEAD_DIM
    row = lax.broadcasted_iota(jnp.int32, (q, q), 0)
    col = lax.broadcasted_iota(jnp.int32, (q, q), 1)
    causal = row >= col
    tri = jnp.where(causal, 1.0, 0.0).astype(BF16)
    lane_head = lax.broadcasted_iota(jnp.int32, (q, qw), 1) // HEAD_DIM
    head_mask = [jnp.where(lane_head == r, 1.0, 0.0).astype(BF16) for r in range(QUAD)]

    da = dt_c * a_row
    acs = sum(jnp.dot(tri, part, preferred_element_type=F32) for part in _split3(da))
    acs_t = acs.T
    dt_t = dt_c.T
    ea = jnp.exp(acs)
    dd = jnp.exp(acs[q - 1:q, :] - acs) * dt_c
    ea_x = _expand_heads(ea, e2_ref)
    dd_x = _expand_heads(dd, e2_ref)
    xd = (xs_c * dd_x).astype(BF16)
    bb = b_c.astype(BF16)
    cb_ = c_c.astype(BF16)

    y_parts = []
    for g in range(N_GROUPS):
        bg = bb[:, g * n:(g + 1) * n]
        cg = cb_[:, g * n:(g + 1) * n]
        cbm = _dot_nt(cg, bg)
        ht_g = ht_ref[:, g * gw:(g + 1) * gw]
        y_off = jnp.dot(cg, ht_g.astype(BF16), preferred_element_type=F32)
        for qd in range(gw // qw):
            lo = g * gw + qd * qw
            h0 = lo // HEAD_DIM
            ws = []
            for r in range(QUAD):
                h = h0 + r
                seg = acs[:, h:h + 1] - acs_t[h:h + 1, :]
                lm = jnp.exp(jnp.where(causal, seg, -jnp.inf))
                ws.append((cbm * lm * dt_t[h:h + 1, :]).astype(BF16))
            wcat = jnp.concatenate(ws, axis=1)
            xq = xs_c[:, lo:lo + qw]
            xq_b = xq.astype(BF16)
            rhs = jnp.concatenate([xq_b * head_mask[r] for r in range(QUAD)], axis=0)
            y_diag = jnp.dot(wcat, rhs, preferred_element_type=F32)
            y_parts.append(y_diag + y_off[:, qd * qw:(qd + 1) * qw] * ea_x[:, lo:lo + qw]
                           + d_row[:, lo:lo + qw] * xq)
        bg_t = b_c[:, g * n:(g + 1) * n].T.astype(BF16)
        st = jnp.dot(bg_t, xd[:, g * gw:(g + 1) * gw], preferred_element_type=F32)
        ht_ref[:, g * gw:(g + 1) * gw] = ht_g * ea_x[q - 1:q, g * gw:(g + 1) * gw] + st
    return jnp.concatenate(y_parts, axis=1)


def _prompt_mixer_kernel(x_ref, mod_ref, n1g_ref, w_abc_ref, w_z_ref, w_xbc_ref, w_dt_ref, w_g_ref,
                         caw_ref, cbw_ref, cbb_ref, dtb_ref, alog_ref, dexp_ref, sng_ref,
                         w_ao_ref, w_bo_ref, w_o_ref, e2_ref,
                         x1_ref, na_ref, nbc_ref, nssm_ref,
                         cbuf, xbuf, ybuf, ht_ref):
    t = pl.program_id(1)
    nt = pl.num_programs(1)
    tq = x_ref.shape[1]
    d = x_ref.shape[2]
    d_inner = w_z_ref.shape[1]
    pad = 8

    @pl.when(t == 0)
    def _():
        cbuf[0:pad, :] = jnp.zeros((pad, cbuf.shape[1]), F32)
        xbuf[0:pad, :] = jnp.zeros((pad, xbuf.shape[1]), F32)
        ht_ref[...] = jnp.zeros(ht_ref.shape, F32)

    x = x_ref[0]
    sh1 = mod_ref[0, :, 0:d]
    sc1 = mod_ref[0, :, d:2 * d]
    g1 = mod_ref[0, :, 2 * d:3 * d]
    ub = (_rms(x, n1g_ref[...]) * (1.0 + sc1) + sh1).astype(BF16)

    def proj(w_ref, lo=None, hi=None):
        w = w_ref[...] if lo is None else w_ref[:, lo:hi]
        return jnp.dot(ub, w, preferred_element_type=F32)

    xbc = proj(w_xbc_ref)
    dt = _softplus(proj(w_dt_ref) + dtb_ref[...])
    a_row = -jnp.exp(alog_ref[...])
    xc, new_bc = _tile_causal_conv(xbuf, xbc, cbw_ref, pad)
    cgate = proj(w_abc_ref, d, 2 * d)
    hval = proj(w_abc_ref, 2 * d, 3 * d)
    xc = _silu(xc + cbb_ref[...])
    bgate = proj(w_abc_ref, 0, d)
    conv, new_a = _tile_causal_conv(cbuf, cgate * hval, caw_ref, pad)

    gn = N_GROUPS * D_STATE
    fillers = [lambda: proj(w_z_ref), lambda: _bdot(bgate * conv, w_ao_ref[...])]
    filled = []
    for c in range(tq // SSD_CHUNK):
        s = slice(c * SSD_CHUNK, (c + 1) * SSD_CHUNK)
        if c < len(fillers):
            filled.append(fillers[c]())
        ybuf[s, :] = _ssd_chunk(xc[s, 0:d_inner], xc[s, d_inner:d_inner + gn],
                                xc[s, d_inner + gn:d_inner + 2 * gn], dt[s, :], a_row,
                                dexp_ref[...], ht_ref, e2_ref)
    filled += [f() for f in fillers[len(filled):]]
    z, y_a = filled

    merged = jax.nn.sigmoid(proj(w_g_ref, 0, d)) * y_a
    y_b = _bdot(_group_rmsnorm(ybuf[...] * _silu(z), sng_ref[...]), w_bo_ref[...])
    merged = merged + jax.nn.sigmoid(proj(w_g_ref, d, 2 * d)) * y_b
    x1_ref[0] = x + g1 * _bdot(merged, w_o_ref[...])

    @pl.when(t == nt - 1)
    def _():
        na_ref[0] = new_a
        nbc_ref[0] = new_bc
        h = ht_ref[...].T
        nssm_ref[0] = h.reshape(nssm_ref.shape[1:])


def _prompt_mixer(x, mod_p, n1g, w_abc, w_z, w_xbc, w_dt, w_g, caw, cbw, cbb, dtb, alog, dexp, sng,
                  w_ao, w_bo, w_o, e2, *, tq):
    b, l, d = x.shape
    d_inner = w_z.shape[1]
    d_xbc = w_xbc.shape[1]
    n_heads = d_inner // HEAD_DIM
    consts = (n1g, w_abc, w_z, w_xbc, w_dt, w_g, caw, cbw, cbb, dtb, alog, dexp, sng, w_ao, w_bo, w_o, e2)
    return pl.pallas_call(
        _prompt_mixer_kernel,
        grid=(b, l // tq),
        in_specs=[pl.BlockSpec((1, tq, d), lambda i, t: (i, t, 0)),
                  pl.BlockSpec((1, 1, mod_p.shape[2]), lambda i, t: (i, 0, 0))]
                 + [_full_spec(c.shape) for c in consts],
        out_specs=[pl.BlockSpec((1, tq, d), lambda i, t: (i, t, 0)),
                   pl.BlockSpec((1, caw.shape[0] - 1, d), lambda i, t: (i, 0, 0)),
                   pl.BlockSpec((1, cbw.shape[0] - 1, d_xbc), lambda i, t: (i, 0, 0)),
                   pl.BlockSpec((1, n_heads, HEAD_DIM, D_STATE), lambda i, t: (i, 0, 0, 0))],
        out_shape=[jax.ShapeDtypeStruct((b, l, d), F32),
                   jax.ShapeDtypeStruct((b, caw.shape[0] - 1, d), F32),
                   jax.ShapeDtypeStruct((b, cbw.shape[0] - 1, d_xbc), F32),
                   jax.ShapeDtypeStruct((b, n_heads, HEAD_DIM, D_STATE), F32)],
        scratch_shapes=[pltpu.VMEM((tq + 8, d), F32),
                        pltpu.VMEM((tq + 8, d_xbc), F32),
                        pltpu.VMEM((tq, d_inner), F32),
                        pltpu.VMEM((D_STATE, d_inner), F32)],
        compiler_params=pltpu.CompilerParams(
            dimension_semantics=("arbitrary", "arbitrary"), vmem_limit_bytes=VMEM_LIMIT),
        name="prompt_mixer",
    )(x, mod_p, *consts)


def _state_update(j, seq, dec_ref, h0_ref, dtx_ref, bm_ref, cm_ref, hn_ref, y_ref):
    _, n_heads, p, _ = h0_ref.shape
    per = n_heads // N_GROUPS
    eye = jnp.where(lax.broadcasted_iota(jnp.int32, (p, p), 0)
                    == lax.broadcasted_iota(jnp.int32, (p, p), 1), 1.0, 0.0).astype(BF16)
    x_hp = dtx_ref[j]
    x_t = sum(_dot_nt(eye, part) for part in _split3(x_hp))
    for g in range(N_GROUPS):
        brow = bm_ref[j, g:g + 1, :]
        crow = cm_ref[j, g:g + 1, :]
        hs = []
        for r in range(per):
            h = g * per + r
            hn = dec_ref[seq * n_heads + h] * h0_ref[j, h] + x_t[:, h:h + 1] * brow
            hn_ref[j, h] = hn
            hs.append(hn.astype(BF16))
        hg = jnp.concatenate(hs, axis=0)
        cpad = jnp.broadcast_to(crow, (8, crow.shape[1])).astype(BF16)
        y_ref[j, g:g + 1, :] = _dot_nt(cpad, hg)[0:1, :]


def _prompt_mlp_kernel(dec_ref, x_ref, mod_ref, n2g_ref, nfg_ref, w1_ref, w2_ref,
                       h0_ref, dtx_ref, bm_ref, cm_ref, o_ref, hn_ref, y_ref):
    i = pl.program_id(0)
    d = x_ref.shape[1]
    bb = h0_ref.shape[0]
    n_blk = w1_ref.shape[1] // MLP_BLOCK
    sh2 = mod_ref[0, :, 3 * d:4 * d]
    sc2 = mod_ref[0, :, 4 * d:5 * d]
    g2 = mod_ref[0, :, 5 * d:6 * d]

    def between(blk):
        for j in range(blk * bb // n_blk, (blk + 1) * bb // n_blk):
            _state_update(j, i * bb + j, dec_ref, h0_ref, dtx_ref, bm_ref, cm_ref, hn_ref, y_ref)

    x2 = _mlp(x_ref[...], sh2, sc2, g2, n2g_ref[...], w1_ref, w2_ref, between)
    o_ref[...] = _rms(x2, nfg_ref[...])


def _prompt_mlp_with_state(x1, mod_p, n2g, nfg, w1, w2, dec_flat, h0, dtx3, bm3, cm3, *, tm,
                           rows_per_seq):
    t, d = x1.shape
    per = rows_per_seq // tm
    steps = t // tm
    n, n_heads, p, ns = h0.shape
    assert n % steps == 0
    bb = n // steps
    gw = n_heads // N_GROUPS * p

    def full(shape):
        nd = len(shape)
        return pl.BlockSpec(shape, lambda i, s: (0,) * nd, pipeline_mode=pl.Buffered(1))

    return pl.pallas_call(
        _prompt_mlp_kernel,
        grid_spec=pltpu.PrefetchScalarGridSpec(
            num_scalar_prefetch=1,
            grid=(steps,),
            in_specs=[pl.BlockSpec((tm, d), lambda i, s: (i, 0)),
                      pl.BlockSpec((1, 1, mod_p.shape[2]), lambda i, s: (i // per, 0, 0)),
                      full(n2g.shape), full(nfg.shape), full(w1.shape), full(w2.shape),
                      pl.BlockSpec((bb, n_heads, p, ns), lambda i, s: (i, 0, 0, 0)),
                      pl.BlockSpec((bb, n_heads, p), lambda i, s: (i, 0, 0)),
                      pl.BlockSpec((bb, N_GROUPS, ns), lambda i, s: (i, 0, 0)),
                      pl.BlockSpec((bb, N_GROUPS, ns), lambda i, s: (i, 0, 0))],
            out_specs=[pl.BlockSpec((tm, d), lambda i, s: (i, 0)),
                       pl.BlockSpec((bb, n_heads, p, ns), lambda i, s: (i, 0, 0, 0)),
                       pl.BlockSpec((bb, N_GROUPS, gw), lambda i, s: (i, 0, 0))]),
        out_shape=[jax.ShapeDtypeStruct((t, d), F32),
                   jax.ShapeDtypeStruct(h0.shape, F32),
                   jax.ShapeDtypeStruct((n, N_GROUPS, gw), F32)],
        compiler_params=pltpu.CompilerParams(
            dimension_semantics=("arbitrary",), vmem_limit_bytes=VMEM_LIMIT),
        name="prompt_mlp",
    )(dec_flat, x1, mod_p, n2g, nfg, w1, w2, h0, dtx3, bm3, cm3)


def _sample_pre_kernel(x_ref, mod_ref, sa_ref, sbc_ref, n1g_ref, w_abc_ref, w_z_ref, w_xbc_ref,
                       w_dt_ref, w_g_ref, caw_ref, cbw_ref, cbb_ref, dtb_ref, alog_ref, w_ao_ref,
                       e2_ref,
                       ma_ref, na_ref, nbc_ref, dtx_ref, xs_ref, bm_ref, cm_ref, dec_ref, sz_ref,
                       sgb_ref):
    d = x_ref.shape[1]
    d_inner = w_z_ref.shape[1]
    d_xbc = w_xbc_ref.shape[1]
    ka = caw_ref.shape[0]
    kb = cbw_ref.shape[0]
    x = x_ref[...]
    sh1 = mod_ref[:, 0:d]
    sc1 = mod_ref[:, d:2 * d]
    ub = (_rms(x, n1g_ref[...]) * (1.0 + sc1) + sh1).astype(BF16)

    bgate = jnp.dot(ub, w_abc_ref[:, 0:d], preferred_element_type=F32)
    cgate = jnp.dot(ub, w_abc_ref[:, d:2 * d], preferred_element_type=F32)
    hval = jnp.dot(ub, w_abc_ref[:, 2 * d:3 * d], preferred_element_type=F32)
    ch = cgate * hval
    conv = ch * caw_ref[ka - 1:ka, :]
    for k in range(ka - 1):
        conv = conv + sa_ref[:, k * d:(k + 1) * d] * caw_ref[k:k + 1, :]
    na_ref[:, 0:(ka - 2) * d] = sa_ref[:, d:(ka - 1) * d]
    na_ref[:, (ka - 2) * d:(ka - 1) * d] = ch
    y_a = _bdot(bgate * conv, w_ao_ref[...])
    ma_ref[...] = jax.nn.sigmoid(jnp.dot(ub, w_g_ref[:, 0:d], preferred_element_type=F32)) * y_a
    sgb_ref[...] = jax.nn.sigmoid(jnp.dot(ub, w_g_ref[:, d:2 * d], preferred_element_type=F32))

    xbc = jnp.dot(ub, w_xbc_ref[...], preferred_element_type=F32)
    xc = xbc * cbw_ref[kb - 1:kb, :]
    for k in range(kb - 1):
        xc = xc + sbc_ref[:, k * d_xbc:(k + 1) * d_xbc] * cbw_ref[k:k + 1, :]
    xc = _silu(xc + cbb_ref[...])
    nbc_ref[:, 0:(kb - 2) * d_xbc] = sbc_ref[:, d_xbc:(kb - 1) * d_xbc]
    nbc_ref[:, (kb - 2) * d_xbc:(kb - 1) * d_xbc] = xbc

    gn = N_GROUPS * D_STATE
    xs = xc[:, 0:d_inner]
    dt = _softplus(jnp.dot(ub, w_dt_ref[...], preferred_element_type=F32) + dtb_ref[...])
    dec_ref[...] = jnp.exp(dt * (-jnp.exp(alog_ref[...])))
    hi, mid, lo = _split3(dt)
    e3 = e2_ref[0:LANES, :]
    dt_x = (jnp.dot(hi, e3, preferred_element_type=F32) + jnp.dot(mid, e3, preferred_element_type=F32)
            + jnp.dot(lo, e3, preferred_element_type=F32))
    dtx_ref[...] = xs * dt_x
    xs_ref[...] = xs
    bm_ref[...] = xc[:, d_inner:d_inner + gn]
    cm_ref[...] = xc[:, d_inner + gn:d_inner + 2 * gn]
    sz_ref[...] = _silu(jnp.dot(ub, w_z_ref[...], preferred_element_type=F32))


def _sample_pre(x, mod_s, sa, sbc, n1g, w_abc, w_z, w_xbc, w_dt, w_g, caw, cbw, cbb, dtb, alog, w_ao, e2):
    n, d = x.shape
    d_inner = w_z.shape[1]
    gn = N_GROUPS * D_STATE
    shapes = [(n, d), sa.shape, sbc.shape, (n, d_inner), (n, d_inner), (n, gn), (n, gn), (n, LANES),
              (n, d_inner), (n, d)]
    return pl.pallas_call(
        _sample_pre_kernel,
        out_shape=[jax.ShapeDtypeStruct(s, F32) for s in shapes],
        compiler_params=pltpu.CompilerParams(vmem_limit_bytes=VMEM_LIMIT),
        name="sample_pre",
    )(x, mod_s, sa, sbc, n1g, w_abc, w_z, w_xbc, w_dt, w_g, caw, cbw, cbb, dtb, alog, w_ao, e2)


def _sample_post_kernel(x_ref, mod_ref, y_ref, xs_ref, sz_ref, ma_ref, sgb_ref, dexp_ref, sng_ref,
                        w_bo_ref, w_o_ref, n2g_ref, nfg_ref, w1_ref, w2_ref, o_ref):
    d = x_ref.shape[1]
    g1 = mod_ref[:, 2 * d:3 * d]
    sh2 = mod_ref[:, 3 * d:4 * d]
    sc2 = mod_ref[:, 4 * d:5 * d]
    g2 = mod_ref[:, 5 * d:6 * d]
    ys = (y_ref[...] + dexp_ref[...] * xs_ref[...]) * sz_ref[...]
    y_b = _bdot(_group_rmsnorm(ys, sng_ref[...]), w_bo_ref[...])
    merged = ma_ref[...] + sgb_ref[...] * y_b
    x1 = x_ref[...] + g1 * _bdot(merged, w_o_ref[...])
    x2 = _mlp(x1, sh2, sc2, g2, n2g_ref[...], w1_ref, w2_ref)
    o_ref[...] = _rms(x2, nfg_ref[...])


def _sample_post(x, mod_s, y, xs, sz, ma, sgb, dexp, sng, w_bo, w_o, n2g, nfg, w1, w2):
    return pl.pallas_call(
        _sample_post_kernel,
        out_shape=jax.ShapeDtypeStruct(x.shape, F32),
        compiler_params=pltpu.CompilerParams(vmem_limit_bytes=VMEM_LIMIT),
        name="sample_post",
    )(x, mod_s, y, xs, sz, ma, sgb, dexp, sng, w_bo, w_o, n2g, nfg, w1, w2)


def _layer_weights(l, w_ada, b_ada, norm1_g, w_in, conv_a_w, w_a_out, conv_b_w, conv_b_b, dt_bias,
                   a_log, d_skip, ssm_norm_g, w_b_out, w_o, norm2_g, w_mlp1, w_mlp2):
    d = w_in.shape[1]
    d_inner = w_b_out.shape[1]
    d_xbc = conv_b_w.shape[2]
    n_heads = dt_bias.shape[1]
    o_z = 3 * d
    o_xbc = o_z + d_inner
    o_dt = o_xbc + d_xbc
    o_g = o_dt + n_heads
    wi = w_in[l]
    row = lambda v: v.reshape(1, -1)
    lane_pad = lambda v: jnp.pad(v, ((0, 0), (0, LANES - v.shape[1])))
    return dict(
        w_ada=w_ada[l].astype(BF16), b_ada=row(b_ada[l]), n1g=row(norm1_g[l]),
        w_abc=wi[:, 0:o_z].astype(BF16), w_z=wi[:, o_z:o_xbc].astype(BF16),
        w_xbc=wi[:, o_xbc:o_dt].astype(BF16), w_dt=lane_pad(wi[:, o_dt:o_g]).astype(BF16),
        w_g=wi[:, o_g:o_g + 2 * d].astype(BF16),
        caw=conv_a_w[l], cbw=conv_b_w[l], cbb=row(conv_b_b[l]),
        dtb=lane_pad(row(dt_bias[l])), alog=lane_pad(row(a_log[l])),
        dexp=row(jnp.repeat(d_skip[l], HEAD_DIM)), sng=row(ssm_norm_g[l]),
        w_ao=w_a_out[l].astype(BF16), w_bo=w_b_out[l].astype(BF16), w_o=w_o[l].astype(BF16),
        n2g=row(norm2_g[l]), w1=w_mlp1[l].astype(BF16), w2=w_mlp2[l].astype(BF16))


def kernel(x_prompt, x_sample, c_prompt, c_sample, state_shortconv, state_ssm_conv, state_ssm, w_ada, b_ada, norm1_g, w_in, conv_a_w, w_a_out, conv_b_w, conv_b_b, dt_bias, a_log, d_skip, ssm_norm_g, w_b_out, w_o, norm2_g, w_mlp1, w_mlp2, norm_f_g):
    bp, lp, d = x_prompt.shape
    bs, ls, _ = x_sample.shape
    assert ls == 1
    depth = w_in.shape[0]
    d_inner = w_b_out.shape[1]
    n_heads = dt_bias.shape[1]
    assert n_heads <= LANES and d_inner == n_heads * HEAD_DIM

    e1 = (jnp.arange(LANES)[:, None] == (jnp.arange(d_inner)[None, :] // HEAD_DIM)).astype(BF16)
    e2 = jnp.concatenate([e1, e1], axis=0)
    nfg = norm_f_g.reshape(1, -1)

    assert depth == 1
    w = _layer_weights(0, w_ada, b_ada, norm1_g, w_in, conv_a_w, w_a_out, conv_b_w, conv_b_b,
                       dt_bias, a_log, d_skip, ssm_norm_g, w_b_out, w_o, norm2_g, w_mlp1, w_mlp2)
    mod = _modulation(jnp.concatenate([c_prompt, c_sample], axis=0), w["w_ada"], w["b_ada"])
    mod_p = mod[:bp].reshape(bp, 1, -1)
    mod_s = mod[bp:]

    xs0 = x_sample.reshape(bs, d)
    sa = state_shortconv[0].reshape(bs, -1)
    sbc = state_ssm_conv[0].reshape(bs, -1)
    ma, na_s, nbc_s, dtx, xs, bm, cm, dec, sz, sgb = _sample_pre(
        xs0, mod_s, sa, sbc, w["n1g"], w["w_abc"], w["w_z"], w["w_xbc"], w["w_dt"], w["w_g"],
        w["caw"], w["cbw"], w["cbb"], w["dtb"], w["alog"], w["w_ao"], e2)

    x1, na, nbc, nssm = _prompt_mixer(
        x_prompt, mod_p, w["n1g"], w["w_abc"], w["w_z"], w["w_xbc"], w["w_dt"], w["w_g"], w["caw"],
        w["cbw"], w["cbb"], w["dtb"], w["alog"], w["dexp"], w["sng"], w["w_ao"], w["w_bo"],
        w["w_o"], e2, tq=256)
    yp, hn, y3 = _prompt_mlp_with_state(
        x1.reshape(bp * lp, d), mod_p, w["n2g"], nfg, w["w1"], w["w2"],
        dec[:, :n_heads].reshape(-1), state_ssm[0], dtx.reshape(bs, n_heads, HEAD_DIM),
        bm.reshape(bs, N_GROUPS, D_STATE), cm.reshape(bs, N_GROUPS, D_STATE),
        tm=512, rows_per_seq=lp)

    ys = _sample_post(xs0, mod_s, y3.reshape(bs, d_inner), xs, sz, ma, sgb, w["dexp"], w["sng"],
                      w["w_bo"], w["w_o"], w["n2g"], nfg, w["w1"], w["w2"])

    return (yp.reshape(bp, lp, d), ys.reshape(bs, ls, d), na[None], nbc[None], nssm[None],
            na_s.reshape(state_shortconv.shape), nbc_s.reshape(state_ssm_conv.shape), hn[None])
```

```python
import functools

import jax
import jax.numpy as jnp
from jax import lax
from jax.experimental import pallas as pl
from jax.experimental.pallas import tpu as pltpu

F32 = jnp.float32
BF16 = jnp.bfloat16

EPS = 1e-6
HEAD_DIM = 64
N_GROUPS = 4
D_STATE = 128
LANES = 128
SSD_CHUNK = 128
QUAD = 4
MLP_BLOCK = 1024
VMEM_LIMIT = 56 * 1024 * 1024


def _full_spec(shape):
    nd = len(shape)
    return pl.BlockSpec(shape, lambda *_: (0,) * nd, pipeline_mode=pl.Buffered(1))


def _bdot(a, b):
    return jnp.dot(a.astype(BF16), b, preferred_element_type=F32)


def _dot_nt(a, b):
    return lax.dot_general(a, b, (((1,), (1,)), ((), ())), preferred_element_type=F32)


def _rms(x, g):
    ms = jnp.mean(x * x, axis=-1, keepdims=True)
    return x * lax.rsqrt(ms + EPS) * g


def _silu(x):
    return x * jax.nn.sigmoid(x)


def _softplus(x):
    return jnp.maximum(x, 0.0) + jnp.log1p(jnp.exp(-jnp.abs(x)))


def _split2(v):
    hi = v.astype(BF16)
    lo = (v - hi.astype(F32)).astype(BF16)
    return hi, lo


def _split3(v):
    hi = v.astype(BF16)
    r = v - hi.astype(F32)
    mid = r.astype(BF16)
    lo = (r - mid.astype(F32)).astype(BF16)
    return hi, mid, lo


def _expand_heads(v, e2_ref):
    hi, lo = _split2(v)
    return jnp.dot(jnp.concatenate([hi, lo], axis=1), e2_ref[...], preferred_element_type=F32)


def _group_rmsnorm(y, g):
    d = y.shape[-1]
    w = d // N_GROUPS
    parts = []
    for k in range(N_GROUPS):
        yk = y[:, k * w:(k + 1) * w]
        ms = jnp.mean(yk * yk, axis=-1, keepdims=True)
        parts.append(yk * lax.rsqrt(ms + EPS))
    return jnp.concatenate(parts, axis=1) * g


def _mlp(x, sh2, sc2, g2, n2g, w1_ref, w2_ref, between=None):
    d_ff = w1_ref.shape[1]
    blk = MLP_BLOCK
    ub = (_rms(x, n2g) * (1.0 + sc2) + sh2).astype(BF16)
    acc = None
    for j in range(d_ff // blk):
        if between is not None:
            between(j)
        h = jnp.dot(ub, w1_ref[:, j * blk:(j + 1) * blk], preferred_element_type=F32)
        h = jnp.square(jnp.maximum(h, 0.0))
        p = _bdot(h, w2_ref[j * blk:(j + 1) * blk, :])
        acc = p if acc is None else acc + p
    return x + g2 * acc


def _mod_kernel(c_ref, w_ref, b_ref, o_ref):
    c = c_ref[...]
    o_ref[...] = _bdot(_silu(c), w_ref[...]) + b_ref[...]


def _modulation(c_all, w_ada, b_ada):
    n, d = c_all.shape
    dm = w_ada.shape[1]
    blk = 1024
    return pl.pallas_call(
        _mod_kernel,
        grid=(dm // blk,),
        in_specs=[pl.BlockSpec((n, d), lambda j: (0, 0)),
                  pl.BlockSpec((d, blk), lambda j: (0, j)),
                  pl.BlockSpec((1, blk), lambda j: (0, j))],
        out_specs=pl.BlockSpec((n, blk), lambda j: (0, j)),
        out_shape=jax.ShapeDtypeStruct((n, dm), F32),
        name="modulation",
    )(c_all, w_ada, b_ada)


def _tile_causal_conv(buf, cur, w_ref, pad):
    k_w = w_ref.shape[0]
    tq = cur.shape[0]
    buf[pad:pad + tq, :] = cur
    out = cur * w_ref[k_w - 1:k_w, :]
    for k in range(k_w - 1):
        lo = pad - (k_w - 1) + k
        out = out + buf[lo:lo + tq, :] * w_ref[k:k + 1, :]
    last = buf[pad + tq - (k_w - 1):pad + tq, :]
    buf[pad - (k_w - 1):pad, :] = last
    return out, last


def _ssd_chunk(xs_c, b_c, c_c, dt_c, a_row, d_row, ht_ref, e2_ref):
    q = xs_c.shape[0]
    n = D_STATE
    gw = xs_c.shape[1] // N_GROUPS
    qw = QUAD * HEAD_DIM
    row = lax.broadcasted_iota(jnp.int32, (q, q), 0)
    col = lax.broadcasted_iota(jnp.int32, (q, q), 1)
    causal = row >= col
    tri = jnp.where(causal, 1.0, 0.0).astype(BF16)
    lane_head = lax.broadcasted_iota(jnp.int32, (q, qw), 1) // HEAD_DIM
    head_mask = [jnp.where(lane_head == r, 1.0, 0.0).astype(BF16) for r in range(QUAD)]

    da = dt_c * a_row
    acs = sum(jnp.dot(tri, part, preferred_element_type=F32) for part in _split3(da))
    acs_t = acs.T
    dt_t = dt_c.T
    ea = jnp.exp(acs)
    dd = jnp.exp(acs[q - 1:q, :] - acs) * dt_c
    ea_x = _expand_heads(ea, e2_ref)
    dd_x = _expand_heads(dd, e2_ref)
    xd = (xs_c * dd_x).astype(BF16)
    bb = b_c.astype(BF16)
    cb_ = c_c.astype(BF16)

    y_parts = []
    for g in range(N_GROUPS):
        bg = bb[:, g * n:(g + 1) * n]
        cg = cb_[:, g * n:(g + 1) * n]
        cbm = _dot_nt(cg, bg)
        ht_g = ht_ref[:, g * gw:(g + 1) * gw]
        y_off = jnp.dot(cg, ht_g.astype(BF16), preferred_element_type=F32)
        for qd in range(gw // qw):
            lo = g * gw + qd * qw
            h0 = lo // HEAD_DIM
            ws = []
            for r in range(QUAD):
                h = h0 + r
                seg = acs[:, h:h + 1] - acs_t[h:h + 1, :]
                lm = jnp.exp(jnp.where(causal, seg, -jnp.inf))
                ws.append((cbm * lm * dt_t[h:h + 1, :]).astype(BF16))
            wcat = jnp.concatenate(ws, axis=1)
            xq = xs_c[:, lo:lo + qw]
            xq_b = xq.astype(BF16)
            rhs = jnp.concatenate([xq_b * head_mask[r] for r in range(QUAD)], axis=0)
            y_diag = jnp.dot(wcat, rhs, preferred_element_type=F32)
            y_parts.append(y_diag + y_off[:, qd * qw:(qd + 1) * qw] * ea_x[:, lo:lo + qw]
                           + d_row[:, lo:lo + qw] * xq)
        bg_t = b_c[:, g * n:(g + 1) * n].T.astype(BF16)
        st = jnp.dot(bg_t, xd[:, g * gw:(g + 1) * gw], preferred_element_type=F32)
        ht_ref[:, g * gw:(g + 1) * gw] = ht_g * ea_x[q - 1:q, g * gw:(g + 1) * gw] + st
    return jnp.concatenate(y_parts, axis=1)


def _prompt_mixer_kernel(x_ref, mod_ref, n1g_ref, w_abc_ref, w_z_ref, w_xbc_ref, w_dt_ref, w_g_ref,
                         caw_ref, cbw_ref, cbb_ref, dtb_ref, alog_ref, dexp_ref, sng_ref,
                         w_ao_ref, w_bo_ref, w_o_ref, e2_ref,
                         x1_ref, na_ref, nbc_ref, nssm_ref,
                         cbuf, xbuf, ybuf, ht_ref):
    t = pl.program_id(1)
    nt = pl.num_programs(1)
    tq = x_ref.shape[1]
    d = x_ref.shape[2]
    d_inner = w_z_ref.shape[1]
    pad = 8

    @pl.when(t == 0)
    def _():
        cbuf[0:pad, :] = jnp.zeros((pad, cbuf.shape[1]), F32)
        xbuf[0:pad, :] = jnp.zeros((pad, xbuf.shape[1]), F32)
        ht_ref[...] = jnp.zeros(ht_ref.shape, F32)

    x = x_ref[0]
    sh1 = mod_ref[0, :, 0:d]
    sc1 = mod_ref[0, :, d:2 * d]
    g1 = mod_ref[0, :, 2 * d:3 * d]
    ub = (_rms(x, n1g_ref[...]) * (1.0 + sc1) + sh1).astype(BF16)

    def proj(w_ref, lo=None, hi=None):
        w = w_ref[...] if lo is None else w_ref[:, lo:hi]
        return jnp.dot(ub, w, preferred_element_type=F32)

    xbc = proj(w_xbc_ref)
    dt = _softplus(proj(w_dt_ref) + dtb_ref[...])
    a_row = -jnp.exp(alog_ref[...])
    xc, new_bc = _tile_causal_conv(xbuf, xbc, cbw_ref, pad)
    cgate = proj(w_abc_ref, d, 2 * d)
    hval = proj(w_abc_ref, 2 * d, 3 * d)
    xc = _silu(xc + cbb_ref[...])
    bgate = proj(w_abc_ref, 0, d)
    conv, new_a = _tile_causal_conv(cbuf, cgate * hval, caw_ref, pad)

    gn = N_GROUPS * D_STATE
    fillers = [lambda: proj(w_z_ref), lambda: _bdot(bgate * conv, w_ao_ref[...])]
    filled = []
    for c in range(tq // SSD_CHUNK):
        s = slice(c * SSD_CHUNK, (c + 1) * SSD_CHUNK)
        if c < len(fillers):
            filled.append(fillers[c]())
        ybuf[s, :] = _ssd_chunk(xc[s, 0:d_inner], xc[s, d_inner:d_inner + gn],
                                xc[s, d_inner + gn:d_inner + 2 * gn], dt[s, :], a_row,
                                dexp_ref[...], ht_ref, e2_ref)
    filled += [f() for f in fillers[len(filled):]]
    z, y_a = filled

    merged = jax.nn.sigmoid(proj(w_g_ref, 0, d)) * y_a
    y_b = _bdot(_group_rmsnorm(ybuf[...] * _silu(z), sng_ref[...]), w_bo_ref[...])
    merged = merged + jax.nn.sigmoid(proj(w_g_ref, d, 2 * d)) * y_b
    x1_ref[0] = x + g1 * _bdot(merged, w_o_ref[...])

    @pl.when(t == nt - 1)
    def _():
        na_ref[0] = new_a
        nbc_ref[0] = new_bc
        h = ht_ref[...].T
        nssm_ref[0] = h.reshape(nssm_ref.shape[1:])


def _prompt_mixer(x, mod_p, n1g, w_abc, w_z, w_xbc, w_dt, w_g, caw, cbw, cbb, dtb, alog, dexp, sng,
                  w_ao, w_bo, w_o, e2, *, tq):
    b, l, d = x.shape
    d_inner = w_z.shape[1]
    d_xbc = w_xbc.shape[1]
    n_heads = d_inner // HEAD_DIM
    consts = (n1g, w_abc, w_z, w_xbc, w_dt, w_g, caw, cbw, cbb, dtb, alog, dexp, sng, w_ao, w_bo, w_o, e2)
    return pl.pallas_call(
        _prompt_mixer_kernel,
        grid=(b, l // tq),
        in_specs=[pl.BlockSpec((1, tq, d), lambda i, t: (i, t, 0)),
                  pl.BlockSpec((1, 1, mod_p.shape[2]), lambda i, t: (i, 0, 0))]
                 + [_full_spec(c.shape) for c in consts],
        out_specs=[pl.BlockSpec((1, tq, d), lambda i, t: (i, t, 0)),
                   pl.BlockSpec((1, caw.shape[0] - 1, d), lambda i, t: (i, 0, 0)),
                   pl.BlockSpec((1, cbw.shape[0] - 1, d_xbc), lambda i, t: (i, 0, 0)),
                   pl.BlockSpec((1, n_heads, HEAD_DIM, D_STATE), lambda i, t: (i, 0, 0, 0))],
        out_shape=[jax.ShapeDtypeStruct((b, l, d), F32),
                   jax.ShapeDtypeStruct((b, caw.shape[0] - 1, d), F32),
                   jax.ShapeDtypeStruct((b, cbw.shape[0] - 1, d_xbc), F32),
                   jax.ShapeDtypeStruct((b, n_heads, HEAD_DIM, D_STATE), F32)],
        scratch_shapes=[pltpu.VMEM((tq + 8, d), F32),
                        pltpu.VMEM((tq + 8, d_xbc), F32),
                        pltpu.VMEM((tq, d_inner), F32),
                        pltpu.VMEM((D_STATE, d_inner), F32)],
        compiler_params=pltpu.CompilerParams(
            dimension_semantics=("arbitrary", "arbitrary"), vmem_limit_bytes=VMEM_LIMIT),
        name="prompt_mixer",
    )(x, mod_p, *consts)


def _state_update_group(g, seq0, dec_ref, h0_ref, dtx_ref, bm_ref, cm_ref, hn_ref, yt_ref):
    bb, n_heads, p, n = h0_ref.shape
    per = n_heads // N_GROUPS
    lane = lax.broadcasted_iota(jnp.int32, (n, yt_ref.shape[1]), 1)
    lhs, rhs = [], []
    for j in range(bb):
        x_t = dtx_ref[j].T
        brow = bm_ref[j, g:g + 1, :]
        hs = []
        for r in range(per):
            h = g * per + r
            hn = dec_ref[(seq0 + j) * n_heads + h] * h0_ref[j, h] + x_t[:, h:h + 1] * brow
            hn_ref[j, h] = hn
            hs.append(hn.astype(BF16))
        lhs.append(jnp.concatenate(hs, axis=0))
        c_col = cm_ref[j].T[:, g:g + 1]
        rhs.append(jnp.where(lane == seq0 + j, c_col, 0.0).astype(BF16))
    rows = slice(g * per * p, (g + 1) * per * p)
    yt_ref[rows, :] += jnp.dot(jnp.concatenate(lhs, axis=1), jnp.concatenate(rhs, axis=0),
                               preferred_element_type=F32)


def _prompt_mlp_kernel(dec_ref, x_ref, mod_ref, n2g_ref, nfg_ref, w1_ref, w2_ref,
                       h0_ref, dtx_ref, bm_ref, cm_ref, o_ref, hn_ref, yt_ref):
    i = pl.program_id(0)
    d = x_ref.shape[1]
    bb = h0_ref.shape[0]
    n_blk = w1_ref.shape[1] // MLP_BLOCK
    sh2 = mod_ref[0, :, 3 * d:4 * d]
    sc2 = mod_ref[0, :, 4 * d:5 * d]
    g2 = mod_ref[0, :, 5 * d:6 * d]

    @pl.when(i == 0)
    def _():
        yt_ref[...] = jnp.zeros(yt_ref.shape, F32)

    def between(blk):
        for g in range(blk * N_GROUPS // n_blk, (blk + 1) * N_GROUPS // n_blk):
            _state_update_group(g, i * bb, dec_ref, h0_ref, dtx_ref, bm_ref, cm_ref, hn_ref, yt_ref)

    x2 = _mlp(x_ref[...], sh2, sc2, g2, n2g_ref[...], w1_ref, w2_ref, between)
    o_ref[...] = _rms(x2, nfg_ref[...])


def _prompt_mlp_with_state(x1, mod_p, n2g, nfg, w1, w2, dec_flat, h0, dtx3, bm3, cm3, *, tm,
                           rows_per_seq):
    t, d = x1.shape
    per = rows_per_seq // tm
    steps = t // tm
    n, n_heads, p, ns = h0.shape
    assert n % steps == 0
    bb = n // steps

    def full(shape):
        nd = len(shape)
        return pl.BlockSpec(shape, lambda i, s: (0,) * nd, pipeline_mode=pl.Buffered(1))

    return pl.pallas_call(
        _prompt_mlp_kernel,
        grid_spec=pltpu.PrefetchScalarGridSpec(
            num_scalar_prefetch=1,
            grid=(steps,),
            in_specs=[pl.BlockSpec((tm, d), lambda i, s: (i, 0)),
                      pl.BlockSpec((1, 1, mod_p.shape[2]), lambda i, s: (i // per, 0, 0)),
                      full(n2g.shape), full(nfg.shape), full(w1.shape), full(w2.shape),
                      pl.BlockSpec((bb, n_heads, p, ns), lambda i, s: (i, 0, 0, 0)),
                      pl.BlockSpec((bb, n_heads, p), lambda i, s: (i, 0, 0)),
                      pl.BlockSpec((bb, N_GROUPS, ns), lambda i, s: (i, 0, 0)),
                      pl.BlockSpec((bb, N_GROUPS, ns), lambda i, s: (i, 0, 0))],
            out_specs=[pl.BlockSpec((tm, d), lambda i, s: (i, 0)),
                       pl.BlockSpec((bb, n_heads, p, ns), lambda i, s: (i, 0, 0, 0)),
                       pl.BlockSpec((n_heads * p, n), lambda i, s: (0, 0))]),
        out_shape=[jax.ShapeDtypeStruct((t, d), F32),
                   jax.ShapeDtypeStruct(h0.shape, F32),
                   jax.ShapeDtypeStruct((n_heads * p, n), F32)],
        compiler_params=pltpu.CompilerParams(
            dimension_semantics=("arbitrary",), vmem_limit_bytes=VMEM_LIMIT),
        name="prompt_mlp",
    )(dec_flat, x1, mod_p, n2g, nfg, w1, w2, h0, dtx3, bm3, cm3)


def _sample_pre_kernel(row0, x_ref, mod_ref, sa_ref, sbc_ref, n1g_ref, w_abc_ref, w_z_ref, w_xbc_ref,
                       w_dt_ref, w_g_ref, caw_ref, cbw_ref, cbb_ref, dtb_ref, alog_ref, w_ao_ref,
                       e2_ref,
                       ma_ref, na_ref, nbc_ref, dtx_ref, xs_ref, bm_ref, cm_ref, dec_ref, sz_ref,
                       sgb_ref):
    d = x_ref.shape[1]
    d_inner = w_z_ref.shape[1]
    d_xbc = w_xbc_ref.shape[1]
    ka = caw_ref.shape[0]
    kb = cbw_ref.shape[0]
    x = x_ref[...]
    rows = slice(row0, row0 + x.shape[0])
    sh1 = mod_ref[rows, 0:d]
    sc1 = mod_ref[rows, d:2 * d]
    ub = (_rms(x, n1g_ref[...]) * (1.0 + sc1) + sh1).astype(BF16)

    bgate = jnp.dot(ub, w_abc_ref[:, 0:d], preferred_element_type=F32)
    cgate = jnp.dot(ub, w_abc_ref[:, d:2 * d], preferred_element_type=F32)
    hval = jnp.dot(ub, w_abc_ref[:, 2 * d:3 * d], preferred_element_type=F32)
    ch = cgate * hval
    conv = ch * caw_ref[ka - 1:ka, :]
    for k in range(ka - 1):
        conv = conv + sa_ref[k] * caw_ref[k:k + 1, :]
    for k in range(ka - 2):
        na_ref[k] = sa_ref[k + 1]
    na_ref[ka - 2] = ch
    y_a = _bdot(bgate * conv, w_ao_ref[...])
    ma_ref[...] = jax.nn.sigmoid(jnp.dot(ub, w_g_ref[:, 0:d], preferred_element_type=F32)) * y_a
    sgb_ref[...] = jax.nn.sigmoid(jnp.dot(ub, w_g_ref[:, d:2 * d], preferred_element_type=F32))

    xbc = jnp.dot(ub, w_xbc_ref[...], preferred_element_type=F32)
    xc = xbc * cbw_ref[kb - 1:kb, :]
    for k in range(kb - 1):
        xc = xc + sbc_ref[k] * cbw_ref[k:k + 1, :]
    xc = _silu(xc + cbb_ref[...])
    for k in range(kb - 2):
        nbc_ref[k] = sbc_ref[k + 1]
    nbc_ref[kb - 2] = xbc

    gn = N_GROUPS * D_STATE
    xs = xc[:, 0:d_inner]
    dt = _softplus(jnp.dot(ub, w_dt_ref[...], preferred_element_type=F32) + dtb_ref[...])
    dec_ref[...] = jnp.exp(dt * (-jnp.exp(alog_ref[...])))
    hi, mid, lo = _split3(dt)
    e3 = e2_ref[0:LANES, :]
    dt_x = (jnp.dot(hi, e3, preferred_element_type=F32) + jnp.dot(mid, e3, preferred_element_type=F32)
            + jnp.dot(lo, e3, preferred_element_type=F32))
    dtx_ref[...] = xs * dt_x
    xs_ref[...] = xs
    bm_ref[...] = xc[:, d_inner:d_inner + gn]
    cm_ref[...] = xc[:, d_inner + gn:d_inner + 2 * gn]
    sz_ref[...] = _silu(jnp.dot(ub, w_z_ref[...], preferred_element_type=F32))


def _sample_pre(x, mod, row0, sa, sbc, n1g, w_abc, w_z, w_xbc, w_dt, w_g, caw, cbw, cbb, dtb, alog, w_ao, e2):
    n, d = x.shape
    d_inner = w_z.shape[1]
    gn = N_GROUPS * D_STATE
    shapes = [(n, d), sa.shape, sbc.shape, (n, d_inner), (n, d_inner), (n, gn), (n, gn), (n, LANES),
              (n, d_inner), (n, d)]
    return pl.pallas_call(
        functools.partial(_sample_pre_kernel, row0),
        out_shape=[jax.ShapeDtypeStruct(s, F32) for s in shapes],
        compiler_params=pltpu.CompilerParams(vmem_limit_bytes=VMEM_LIMIT),
        name="sample_pre",
    )(x, mod, sa, sbc, n1g, w_abc, w_z, w_xbc, w_dt, w_g, caw, cbw, cbb, dtb, alog, w_ao, e2)


def _sample_post_kernel(row0, x_ref, mod_ref, yt_ref, xs_ref, sz_ref, ma_ref, sgb_ref, dexp_ref, sng_ref,
                        w_bo_ref, w_o_ref, n2g_ref, nfg_ref, w1_ref, w2_ref, o_ref):
    d = x_ref.shape[1]
    rows = slice(row0, row0 + x_ref.shape[0])
    g1 = mod_ref[rows, 2 * d:3 * d]
    sh2 = mod_ref[rows, 3 * d:4 * d]
    sc2 = mod_ref[rows, 4 * d:5 * d]
    g2 = mod_ref[rows, 5 * d:6 * d]
    ys = (yt_ref[...].T + dexp_ref[...] * xs_ref[...]) * sz_ref[...]
    y_b = _bdot(_group_rmsnorm(ys, sng_ref[...]), w_bo_ref[...])
    merged = ma_ref[...] + sgb_ref[...] * y_b
    x1 = x_ref[...] + g1 * _bdot(merged, w_o_ref[...])
    x2 = _mlp(x1, sh2, sc2, g2, n2g_ref[...], w1_ref, w2_ref)
    o_ref[...] = _rms(x2, nfg_ref[...])


def _sample_post(x, mod, row0, y, xs, sz, ma, sgb, dexp, sng, w_bo, w_o, n2g, nfg, w1, w2):
    return pl.pallas_call(
        functools.partial(_sample_post_kernel, row0),
        out_shape=jax.ShapeDtypeStruct(x.shape, F32),
        compiler_params=pltpu.CompilerParams(vmem_limit_bytes=VMEM_LIMIT),
        name="sample_post",
    )(x, mod, y, xs, sz, ma, sgb, dexp, sng, w_bo, w_o, n2g, nfg, w1, w2)


def _split_w_in_kernel(bounds, w_ref, abc_ref, z_ref, xbc_ref, dt_ref, g_ref):
    o_abc, o_z, o_xbc, o_dt, o_g, o_end = bounds
    abc_ref[...] = w_ref[:, o_abc:o_z].astype(BF16)
    z_ref[...] = w_ref[:, o_z:o_xbc].astype(BF16)
    xbc_ref[...] = w_ref[:, o_xbc:o_dt].astype(BF16)
    lane = lax.broadcasted_iota(jnp.int32, dt_ref.shape, 1)
    dt_ref[...] = jnp.where(lane < o_g - o_dt, w_ref[:, o_dt:o_dt + LANES], 0.0).astype(BF16)
    g_ref[...] = w_ref[:, o_g:o_end].astype(BF16)


def _split_w_in(w, bounds):
    k, n = w.shape
    o_abc, o_z, o_xbc, o_dt, o_g, o_end = bounds
    assert o_end == n and o_dt + LANES <= n and o_g - o_dt <= LANES
    rb = 128
    widths = (o_z - o_abc, o_xbc - o_z, o_dt - o_xbc, LANES, o_end - o_g)
    return pl.pallas_call(
        functools.partial(_split_w_in_kernel, bounds),
        grid=(k // rb,),
        in_specs=[pl.BlockSpec((rb, n), lambda i: (i, 0))],
        out_specs=[pl.BlockSpec((rb, wd), lambda i: (i, 0)) for wd in widths],
        out_shape=[jax.ShapeDtypeStruct((k, wd), BF16) for wd in widths],
        compiler_params=pltpu.CompilerParams(
            dimension_semantics=("arbitrary",), vmem_limit_bytes=VMEM_LIMIT),
        name="split_w_in",
    )(w)


def _layer_weights(l, w_ada, b_ada, norm1_g, w_in, conv_a_w, w_a_out, conv_b_w, conv_b_b, dt_bias,
                   a_log, d_skip, ssm_norm_g, w_b_out, w_o, norm2_g, w_mlp1, w_mlp2):
    d = w_in.shape[1]
    d_inner = w_b_out.shape[1]
    d_xbc = conv_b_w.shape[2]
    n_heads = dt_bias.shape[1]
    o_z = 3 * d
    o_xbc = o_z + d_inner
    o_dt = o_xbc + d_xbc
    o_g = o_dt + n_heads
    row = lambda v: v.reshape(1, -1)
    lane_pad = lambda v: jnp.pad(v, ((0, 0), (0, LANES - v.shape[1])))
    w_abc, w_z, w_xbc, w_dt, w_g = _split_w_in(w_in[l], (0, o_z, o_xbc, o_dt, o_g, o_g + 2 * d))
    return dict(
        w_ada=w_ada[l].astype(BF16), b_ada=row(b_ada[l]), n1g=row(norm1_g[l]),
        w_abc=w_abc, w_z=w_z, w_xbc=w_xbc, w_dt=w_dt, w_g=w_g,
        caw=conv_a_w[l], cbw=conv_b_w[l], cbb=row(conv_b_b[l]),
        dtb=lane_pad(row(dt_bias[l])), alog=lane_pad(row(a_log[l])),
        dexp=row(jnp.repeat(d_skip[l], HEAD_DIM)), sng=row(ssm_norm_g[l]),
        w_ao=w_a_out[l].astype(BF16), w_bo=w_b_out[l].astype(BF16), w_o=w_o[l].astype(BF16),
        n2g=row(norm2_g[l]), w1=w_mlp1[l].astype(BF16), w2=w_mlp2[l].astype(BF16))


def kernel(x_prompt, x_sample, c_prompt, c_sample, state_shortconv, state_ssm_conv, state_ssm, w_ada, b_ada, norm1_g, w_in, conv_a_w, w_a_out, conv_b_w, conv_b_b, dt_bias, a_log, d_skip, ssm_norm_g, w_b_out, w_o, norm2_g, w_mlp1, w_mlp2, norm_f_g):
    bp, lp, d = x_prompt.shape
    bs, ls, _ = x_sample.shape
    assert ls == 1
    depth = w_in.shape[0]
    d_inner = w_b_out.shape[1]
    n_heads = dt_bias.shape[1]
    assert n_heads <= LANES and d_inner == n_heads * HEAD_DIM

    e1 = (jnp.arange(LANES)[:, None] == (jnp.arange(d_inner)[None, :] // HEAD_DIM)).astype(BF16)
    e2 = jnp.concatenate([e1, e1], axis=0)
    nfg = norm_f_g.reshape(1, -1)

    assert depth == 1
    w = _layer_weights(0, w_ada, b_ada, norm1_g, w_in, conv_a_w, w_a_out, conv_b_w, conv_b_b,
                       dt_bias, a_log, d_skip, ssm_norm_g, w_b_out, w_o, norm2_g, w_mlp1, w_mlp2)
    mod = _modulation(jnp.concatenate([c_prompt, c_sample], axis=0), w["w_ada"], w["b_ada"])
    mod_p = mod[:bp].reshape(bp, 1, -1)

    xs0 = x_sample.reshape(bs, d)
    sa = jnp.transpose(state_shortconv[0], (1, 0, 2))
    sbc = jnp.transpose(state_ssm_conv[0], (1, 0, 2))
    ma, na_s, nbc_s, dtx, xs, bm, cm, dec, sz, sgb = _sample_pre(
        xs0, mod, bp, sa, sbc, w["n1g"], w["w_abc"], w["w_z"], w["w_xbc"], w["w_dt"], w["w_g"],
        w["caw"], w["cbw"], w["cbb"], w["dtb"], w["alog"], w["w_ao"], e2)

    x1, na, nbc, nssm = _prompt_mixer(
        x_prompt, mod_p, w["n1g"], w["w_abc"], w["w_z"], w["w_xbc"], w["w_dt"], w["w_g"], w["caw"],
        w["cbw"], w["cbb"], w["dtb"], w["alog"], w["dexp"], w["sng"], w["w_ao"], w["w_bo"],
        w["w_o"], e2, tq=256)
    yp, hn, yt = _prompt_mlp_with_state(
        x1.reshape(bp * lp, d), mod_p, w["n2g"], nfg, w["w1"], w["w2"],
        dec[:, :n_heads].reshape(-1), state_ssm[0], dtx.reshape(bs, n_heads, HEAD_DIM),
        bm.reshape(bs, N_GROUPS, D_STATE), cm.reshape(bs, N_GROUPS, D_STATE),
        tm=512, rows_per_seq=lp)

    ys = _sample_post(xs0, mod, bp, yt, xs, sz, ma, sgb, w["dexp"], w["sng"],
                      w["w_bo"], w["w_o"], w["n2g"], nfg, w["w1"], w["w2"])

    return (yp.reshape(bp, lp, d), ys.reshape(bs, ls, d), na[None], nbc[None], nssm[None],
            jnp.transpose(na_s, (1, 0, 2))[None], jnp.transpose(nbc_s, (1, 0, 2))[None], hn[None])
```

```python
import functools

import jax
import jax.numpy as jnp
from jax import lax
from jax.experimental import pallas as pl
from jax.experimental.pallas import tpu as pltpu

F32 = jnp.float32
BF16 = jnp.bfloat16

EPS = 1e-6
HEAD_DIM = 64
N_GROUPS = 4
D_STATE = 128
LANES = 128
SSD_CHUNK = 128
QUAD = 4
MLP_BLOCK = 1024
VMEM_LIMIT = 56 * 1024 * 1024


def _full_spec(shape):
    nd = len(shape)
    return pl.BlockSpec(shape, lambda *_: (0,) * nd, pipeline_mode=pl.Buffered(1))


def _bdot(a, b):
    return jnp.dot(a.astype(BF16), b, preferred_element_type=F32)


def _dot_nt(a, b):
    return lax.dot_general(a, b, (((1,), (1,)), ((), ())), preferred_element_type=F32)


def _rms(x, g):
    ms = jnp.mean(x * x, axis=-1, keepdims=True)
    return x * lax.rsqrt(ms + EPS) * g


def _silu(x):
    return x * jax.nn.sigmoid(x)


def _softplus(x):
    return jnp.maximum(x, 0.0) + jnp.log1p(jnp.exp(-jnp.abs(x)))


def _split2(v):
    hi = v.astype(BF16)
    lo = (v - hi.astype(F32)).astype(BF16)
    return hi, lo


def _split3(v):
    hi = v.astype(BF16)
    r = v - hi.astype(F32)
    mid = r.astype(BF16)
    lo = (r - mid.astype(F32)).astype(BF16)
    return hi, mid, lo


def _expand_heads(v, e2_ref):
    hi, lo = _split2(v)
    return jnp.dot(jnp.concatenate([hi, lo], axis=1), e2_ref[...], preferred_element_type=F32)


def _group_rmsnorm(y, g):
    d = y.shape[-1]
    w = d // N_GROUPS
    parts = []
    for k in range(N_GROUPS):
        yk = y[:, k * w:(k + 1) * w]
        ms = jnp.mean(yk * yk, axis=-1, keepdims=True)
        parts.append(yk * lax.rsqrt(ms + EPS))
    return jnp.concatenate(parts, axis=1) * g


def _mlp(x, sh2, sc2, g2, n2g, w1_ref, w2_ref, between=None):
    d_ff = w1_ref.shape[1]
    blk = MLP_BLOCK
    ub = (_rms(x, n2g) * (1.0 + sc2) + sh2).astype(BF16)
    acc = None
    for j in range(d_ff // blk):
        if between is not None:
            between(j)
        h = jnp.dot(ub, w1_ref[:, j * blk:(j + 1) * blk], preferred_element_type=F32)
        h = jnp.square(jnp.maximum(h, 0.0))
        p = _bdot(h, w2_ref[j * blk:(j + 1) * blk, :])
        acc = p if acc is None else acc + p
    return x + g2 * acc


def _mod_kernel(c_ref, w_ref, b_ref, o_ref):
    c = c_ref[...]
    o_ref[...] = _bdot(_silu(c), w_ref[...]) + b_ref[...]


def _modulation(c_all, w_ada, b_ada):
    n, d = c_all.shape
    dm = w_ada.shape[1]
    blk = 1024
    return pl.pallas_call(
        _mod_kernel,
        grid=(dm // blk,),
        in_specs=[pl.BlockSpec((n, d), lambda j: (0, 0)),
                  pl.BlockSpec((d, blk), lambda j: (0, j)),
                  pl.BlockSpec((1, blk), lambda j: (0, j))],
        out_specs=pl.BlockSpec((n, blk), lambda j: (0, j)),
        out_shape=jax.ShapeDtypeStruct((n, dm), F32),
        name="modulation",
    )(c_all, w_ada, b_ada)


def _tile_causal_conv(buf, cur, w_ref, pad):
    k_w = w_ref.shape[0]
    tq = cur.shape[0]
    buf[pad:pad + tq, :] = cur
    out = cur * w_ref[k_w - 1:k_w, :]
    for k in range(k_w - 1):
        lo = pad - (k_w - 1) + k
        out = out + buf[lo:lo + tq, :] * w_ref[k:k + 1, :]
    last = buf[pad + tq - (k_w - 1):pad + tq, :]
    buf[pad - (k_w - 1):pad, :] = last
    return out, last


def _ssd_chunk(xs_c, b_c, c_c, dt_c, a_row, d_row, ht_ref, e2_ref):
    q = xs_c.shape[0]
    n = D_STATE
    gw = xs_c.shape[1] // N_GROUPS
    qw = QUAD * HEAD_DIM
    row = lax.broadcasted_iota(jnp.int32, (q, q), 0)
    col = lax.broadcasted_iota(jnp.int32, (q, q), 1)
    causal = row >= col
    tri = jnp.where(causal, 1.0, 0.0).astype(BF16)
    lane_head = lax.broadcasted_iota(jnp.int32, (q, qw), 1) // HEAD_DIM
    head_mask = [jnp.where(lane_head == r, 1.0, 0.0).astype(BF16) for r in range(QUAD)]

    da = dt_c * a_row
    acs = sum(jnp.dot(tri, part, preferred_element_type=F32) for part in _split3(da))
    acs_t = acs.T
    dt_t = dt_c.T
    ea = jnp.exp(acs)
    dd = jnp.exp(acs[q - 1:q, :] - acs) * dt_c
    ea_x = _expand_heads(ea, e2_ref)
    dd_x = _expand_heads(dd, e2_ref)
    xd = (xs_c * dd_x).astype(BF16)
    bb = b_c.astype(BF16)
    cb_ = c_c.astype(BF16)

    y_parts = []
    for g in range(N_GROUPS):
        bg = bb[:, g * n:(g + 1) * n]
        cg = cb_[:, g * n:(g + 1) * n]
        cbm = _dot_nt(cg, bg)
        ht_g = ht_ref[:, g * gw:(g + 1) * gw]
        y_off = jnp.dot(cg, ht_g.astype(BF16), preferred_element_type=F32)
        for qd in range(gw // qw):
            lo = g * gw + qd * qw
            h0 = lo // HEAD_DIM
            ws = []
            for r in range(QUAD):
                h = h0 + r
                seg = acs[:, h:h + 1] - acs_t[h:h + 1, :]
                lm = jnp.exp(jnp.where(causal, seg, -jnp.inf))
                ws.append((cbm * lm * dt_t[h:h + 1, :]).astype(BF16))
            wcat = jnp.concatenate(ws, axis=1)
            xq = xs_c[:, lo:lo + qw]
            xq_b = xq.astype(BF16)
            rhs = jnp.concatenate([xq_b * head_mask[r] for r in range(QUAD)], axis=0)
            y_diag = jnp.dot(wcat, rhs, preferred_element_type=F32)
            y_parts.append(y_diag + y_off[:, qd * qw:(qd + 1) * qw] * ea_x[:, lo:lo + qw]
                           + d_row[:, lo:lo + qw] * xq)
        bg_t = b_c[:, g * n:(g + 1) * n].T.astype(BF16)
        st = jnp.dot(bg_t, xd[:, g * gw:(g + 1) * gw], preferred_element_type=F32)
        ht_ref[:, g * gw:(g + 1) * gw] = ht_g * ea_x[q - 1:q, g * gw:(g + 1) * gw] + st
    return jnp.concatenate(y_parts, axis=1)


def _prompt_mixer_kernel(x_ref, mod_ref, n1g_ref, w_abc_ref, w_z_ref, w_xbc_ref, w_dt_ref, w_g_ref,
                         caw_ref, cbw_ref, cbb_ref, dtb_ref, alog_ref, dexp_ref, sng_ref,
                         w_ao_ref, w_bo_ref, w_o_ref, e2_ref,
                         x1_ref, na_ref, nbc_ref, nssm_ref,
                         cbuf, xbuf, ybuf, ht_ref):
    t = pl.program_id(1)
    nt = pl.num_programs(1)
    tq = x_ref.shape[1]
    d = x_ref.shape[2]
    d_inner = w_z_ref.shape[1]
    pad = 8

    @pl.when(t == 0)
    def _():
        cbuf[0:pad, :] = jnp.zeros((pad, cbuf.shape[1]), F32)
        xbuf[0:pad, :] = jnp.zeros((pad, xbuf.shape[1]), F32)
        ht_ref[...] = jnp.zeros(ht_ref.shape, F32)

    x = x_ref[0]
    sh1 = mod_ref[0, :, 0:d]
    sc1 = mod_ref[0, :, d:2 * d]
    g1 = mod_ref[0, :, 2 * d:3 * d]
    ub = (_rms(x, n1g_ref[...]) * (1.0 + sc1) + sh1).astype(BF16)

    def proj(w_ref, lo=None, hi=None):
        w = w_ref[...] if lo is None else w_ref[:, lo:hi]
        return jnp.dot(ub, w, preferred_element_type=F32)

    xbc = proj(w_xbc_ref)
    dt = _softplus(proj(w_dt_ref) + dtb_ref[...])
    a_row = -jnp.exp(alog_ref[...])
    xc, new_bc = _tile_causal_conv(xbuf, xbc, cbw_ref, pad)
    cgate = proj(w_abc_ref, d, 2 * d)
    hval = proj(w_abc_ref, 2 * d, 3 * d)
    xc = _silu(xc + cbb_ref[...])
    bgate = proj(w_abc_ref, 0, d)
    conv, new_a = _tile_causal_conv(cbuf, cgate * hval, caw_ref, pad)

    gn = N_GROUPS * D_STATE
    fillers = [lambda: proj(w_z_ref), lambda: _bdot(bgate * conv, w_ao_ref[...])]
    filled = []
    for c in range(tq // SSD_CHUNK):
        s = slice(c * SSD_CHUNK, (c + 1) * SSD_CHUNK)
        if c < len(fillers):
            filled.append(fillers[c]())
        ybuf[s, :] = _ssd_chunk(xc[s, 0:d_inner], xc[s, d_inner:d_inner + gn],
                                xc[s, d_inner + gn:d_inner + 2 * gn], dt[s, :], a_row,
                                dexp_ref[...], ht_ref, e2_ref)
    filled += [f() for f in fillers[len(filled):]]
    z, y_a = filled

    merged = jax.nn.sigmoid(proj(w_g_ref, 0, d)) * y_a
    y_b = _bdot(_group_rmsnorm(ybuf[...] * _silu(z), sng_ref[...]), w_bo_ref[...])
    merged = merged + jax.nn.sigmoid(proj(w_g_ref, d, 2 * d)) * y_b
    x1_ref[0] = x + g1 * _bdot(merged, w_o_ref[...])

    @pl.when(t == nt - 1)
    def _():
        na_ref[0] = new_a
        nbc_ref[0] = new_bc
        h = ht_ref[...].T
        nssm_ref[0] = h.reshape(nssm_ref.shape[1:])


def _prompt_mixer(x, mod_p, n1g, w_abc, w_z, w_xbc, w_dt, w_g, caw, cbw, cbb, dtb, alog, dexp, sng,
                  w_ao, w_bo, w_o, e2, *, tq):
    b, l, d = x.shape
    d_inner = w_z.shape[1]
    d_xbc = w_xbc.shape[1]
    n_heads = d_inner // HEAD_DIM
    consts = (n1g, w_abc, w_z, w_xbc, w_dt, w_g, caw, cbw, cbb, dtb, alog, dexp, sng, w_ao, w_bo, w_o, e2)
    return pl.pallas_call(
        _prompt_mixer_kernel,
        grid=(b, l // tq),
        in_specs=[pl.BlockSpec((1, tq, d), lambda i, t: (i, t, 0)),
                  pl.BlockSpec((1, 1, mod_p.shape[2]), lambda i, t: (i, 0, 0))]
                 + [_full_spec(c.shape) for c in consts],
        out_specs=[pl.BlockSpec((1, tq, d), lambda i, t: (i, t, 0)),
                   pl.BlockSpec((1, caw.shape[0] - 1, d), lambda i, t: (i, 0, 0)),
                   pl.BlockSpec((1, cbw.shape[0] - 1, d_xbc), lambda i, t: (i, 0, 0)),
                   pl.BlockSpec((1, n_heads, HEAD_DIM, D_STATE), lambda i, t: (i, 0, 0, 0))],
        out_shape=[jax.ShapeDtypeStruct((b, l, d), F32),
                   jax.ShapeDtypeStruct((b, caw.shape[0] - 1, d), F32),
                   jax.ShapeDtypeStruct((b, cbw.shape[0] - 1, d_xbc), F32),
                   jax.ShapeDtypeStruct((b, n_heads, HEAD_DIM, D_STATE), F32)],
        scratch_shapes=[pltpu.VMEM((tq + 8, d), F32),
                        pltpu.VMEM((tq + 8, d_xbc), F32),
                        pltpu.VMEM((tq, d_inner), F32),
                        pltpu.VMEM((D_STATE, d_inner), F32)],
        compiler_params=pltpu.CompilerParams(
            dimension_semantics=("arbitrary", "arbitrary"), vmem_limit_bytes=VMEM_LIMIT),
        name="prompt_mixer",
    )(x, mod_p, *consts)


def _state_update_group(g, seq0, dec_ref, h0_ref, dtx_ref, bm_ref, cm_ref, hn_ref, yt_ref):
    bb, n_heads, p, n = h0_ref.shape
    per = n_heads // N_GROUPS
    lane = lax.broadcasted_iota(jnp.int32, (n, yt_ref.shape[1]), 1)
    lhs, rhs = [], []
    for j in range(bb):
        x_t = dtx_ref[j].T
        brow = bm_ref[j, g:g + 1, :]
        hs = []
        for r in range(per):
            h = g * per + r
            hn = dec_ref[(seq0 + j) * n_heads + h] * h0_ref[j, h] + x_t[:, h:h + 1] * brow
            hn_ref[j, h] = hn
            hs.append(hn.astype(BF16))
        lhs.append(jnp.concatenate(hs, axis=0))
        c_col = cm_ref[j].T[:, g:g + 1]
        rhs.append(jnp.where(lane == seq0 + j, c_col, 0.0).astype(BF16))
    rows = slice(g * per * p, (g + 1) * per * p)
    yt_ref[rows, :] += jnp.dot(jnp.concatenate(lhs, axis=1), jnp.concatenate(rhs, axis=0),
                               preferred_element_type=F32)


def _prompt_mlp_kernel(dec_ref, x_ref, mod_ref, n2g_ref, nfg_ref, w1_ref, w2_ref,
                       h0_ref, dtx_ref, bm_ref, cm_ref, o_ref, hn_ref, yt_ref):
    i = pl.program_id(0)
    d = x_ref.shape[1]
    bb = h0_ref.shape[0]
    n_blk = w1_ref.shape[1] // MLP_BLOCK
    sh2 = mod_ref[0, :, 3 * d:4 * d]
    sc2 = mod_ref[0, :, 4 * d:5 * d]
    g2 = mod_ref[0, :, 5 * d:6 * d]

    @pl.when(i == 0)
    def _():
        yt_ref[...] = jnp.zeros(yt_ref.shape, F32)

    def between(blk):
        for g in range(blk * N_GROUPS // n_blk, (blk + 1) * N_GROUPS // n_blk):
            _state_update_group(g, i * bb, dec_ref, h0_ref, dtx_ref, bm_ref, cm_ref, hn_ref, yt_ref)

    x2 = _mlp(x_ref[...], sh2, sc2, g2, n2g_ref[...], w1_ref, w2_ref, between)
    o_ref[...] = _rms(x2, nfg_ref[...])


def _prompt_mlp_with_state(x1, mod_p, n2g, nfg, w1, w2, dec_flat, h0, dtx3, bm3, cm3, *, tm,
                           rows_per_seq):
    t, d = x1.shape
    per = rows_per_seq // tm
    steps = t // tm
    n, n_heads, p, ns = h0.shape
    assert n % steps == 0
    bb = n // steps

    def full(shape):
        nd = len(shape)
        return pl.BlockSpec(shape, lambda i, s: (0,) * nd, pipeline_mode=pl.Buffered(1))

    return pl.pallas_call(
        _prompt_mlp_kernel,
        grid_spec=pltpu.PrefetchScalarGridSpec(
            num_scalar_prefetch=1,
            grid=(steps,),
            in_specs=[pl.BlockSpec((tm, d), lambda i, s: (i, 0)),
                      pl.BlockSpec((1, 1, mod_p.shape[2]), lambda i, s: (i // per, 0, 0)),
                      full(n2g.shape), full(nfg.shape), full(w1.shape), full(w2.shape),
                      pl.BlockSpec((bb, n_heads, p, ns), lambda i, s: (i, 0, 0, 0)),
                      pl.BlockSpec((bb, n_heads, p), lambda i, s: (i, 0, 0)),
                      pl.BlockSpec((bb, N_GROUPS, ns), lambda i, s: (i, 0, 0)),
                      pl.BlockSpec((bb, N_GROUPS, ns), lambda i, s: (i, 0, 0))],
            out_specs=[pl.BlockSpec((tm, d), lambda i, s: (i, 0)),
                       pl.BlockSpec((bb, n_heads, p, ns), lambda i, s: (i, 0, 0, 0)),
                       pl.BlockSpec((n_heads * p, n), lambda i, s: (0, 0))]),
        out_shape=[jax.ShapeDtypeStruct((t, d), F32),
                   jax.ShapeDtypeStruct(h0.shape, F32),
                   jax.ShapeDtypeStruct((n_heads * p, n), F32)],
        compiler_params=pltpu.CompilerParams(
            dimension_semantics=("arbitrary",), vmem_limit_bytes=VMEM_LIMIT),
        name="prompt_mlp",
    )(dec_flat, x1, mod_p, n2g, nfg, w1, w2, h0, dtx3, bm3, cm3)


def _sample_pre_kernel(row0, x_ref, mod_ref, sa_ref, sbc_ref, n1g_ref, w_abc_ref, w_z_ref, w_xbc_ref,
                       w_dt_ref, w_g_ref, caw_ref, cbw_ref, cbb_ref, dtb_ref, alog_ref, w_ao_ref,
                       e2_ref,
                       ma_ref, na_ref, nbc_ref, dtx_ref, xs_ref, bm_ref, cm_ref, dec_ref, sz_ref,
                       sgb_ref):
    d = x_ref.shape[1]
    d_inner = w_z_ref.shape[1]
    d_xbc = w_xbc_ref.shape[1]
    ka = caw_ref.shape[0]
    kb = cbw_ref.shape[0]
    x = x_ref[...]
    rows = slice(row0, row0 + x.shape[0])
    sh1 = mod_ref[rows, 0:d]
    sc1 = mod_ref[rows, d:2 * d]
    ub = (_rms(x, n1g_ref[...]) * (1.0 + sc1) + sh1).astype(BF16)

    bgate = jnp.dot(ub, w_abc_ref[:, 0:d], preferred_element_type=F32)
    cgate = jnp.dot(ub, w_abc_ref[:, d:2 * d], preferred_element_type=F32)
    hval = jnp.dot(ub, w_abc_ref[:, 2 * d:3 * d], preferred_element_type=F32)
    ch = cgate * hval
    conv = ch * caw_ref[ka - 1:ka, :]
    for k in range(ka - 1):
        conv = conv + sa_ref[k] * caw_ref[k:k + 1, :]
    for k in range(ka - 2):
        na_ref[k] = sa_ref[k + 1]
    na_ref[ka - 2] = ch
    y_a = _bdot(bgate * conv, w_ao_ref[...])
    ma_ref[...] = jax.nn.sigmoid(jnp.dot(ub, w_g_ref[:, 0:d], preferred_element_type=F32)) * y_a
    sgb_ref[...] = jax.nn.sigmoid(jnp.dot(ub, w_g_ref[:, d:2 * d], preferred_element_type=F32))

    xbc = jnp.dot(ub, w_xbc_ref[...], preferred_element_type=F32)
    xc = xbc * cbw_ref[kb - 1:kb, :]
    for k in range(kb - 1):
        xc = xc + sbc_ref[k] * cbw_ref[k:k + 1, :]
    xc = _silu(xc + cbb_ref[...])
    for k in range(kb - 2):
        nbc_ref[k] = sbc_ref[k + 1]
    nbc_ref[kb - 2] = xbc

    gn = N_GROUPS * D_STATE
    xs = xc[:, 0:d_inner]
    dt = _softplus(jnp.dot(ub, w_dt_ref[...], preferred_element_type=F32) + dtb_ref[...])
    dec_ref[...] = jnp.exp(dt * (-jnp.exp(alog_ref[...])))
    hi, mid, lo = _split3(dt)
    e3 = e2_ref[0:LANES, :]
    dt_x = (jnp.dot(hi, e3, preferred_element_type=F32) + jnp.dot(mid, e3, preferred_element_type=F32)
            + jnp.dot(lo, e3, preferred_element_type=F32))
    dtx_ref[...] = xs * dt_x
    xs_ref[...] = xs
    bm_ref[...] = xc[:, d_inner:d_inner + gn]
    cm_ref[...] = xc[:, d_inner + gn:d_inner + 2 * gn]
    sz_ref[...] = _silu(jnp.dot(ub, w_z_ref[...], preferred_element_type=F32))


def _sample_pre(x, mod, row0, sa, sbc, n1g, w_abc, w_z, w_xbc, w_dt, w_g, caw, cbw, cbb, dtb, alog, w_ao, e2):
    n, d = x.shape
    d_inner = w_z.shape[1]
    gn = N_GROUPS * D_STATE
    shapes = [(n, d), sa.shape, sbc.shape, (n, d_inner), (n, d_inner), (n, gn), (n, gn), (n, LANES),
              (n, d_inner), (n, d)]
    return pl.pallas_call(
        functools.partial(_sample_pre_kernel, row0),
        out_shape=[jax.ShapeDtypeStruct(s, F32) for s in shapes],
        compiler_params=pltpu.CompilerParams(vmem_limit_bytes=VMEM_LIMIT),
        name="sample_pre",
    )(x, mod, sa, sbc, n1g, w_abc, w_z, w_xbc, w_dt, w_g, caw, cbw, cbb, dtb, alog, w_ao, e2)


def _sample_post_kernel(row0, x_ref, mod_ref, yt_ref, xs_ref, sz_ref, ma_ref, sgb_ref, dexp_ref, sng_ref,
                        w_bo_ref, w_o_ref, n2g_ref, nfg_ref, w1_ref, w2_ref, o_ref):
    d = x_ref.shape[1]
    rows = slice(row0, row0 + x_ref.shape[0])
    g1 = mod_ref[rows, 2 * d:3 * d]
    sh2 = mod_ref[rows, 3 * d:4 * d]
    sc2 = mod_ref[rows, 4 * d:5 * d]
    g2 = mod_ref[rows, 5 * d:6 * d]
    ys = (yt_ref[...].T + dexp_ref[...] * xs_ref[...]) * sz_ref[...]
    y_b = _bdot(_group_rmsnorm(ys, sng_ref[...]), w_bo_ref[...])
    merged = ma_ref[...] + sgb_ref[...] * y_b
    x1 = x_ref[...] + g1 * _bdot(merged, w_o_ref[...])
    x2 = _mlp(x1, sh2, sc2, g2, n2g_ref[...], w1_ref, w2_ref)
    o_ref[...] = _rms(x2, nfg_ref[...])


def _sample_post(x, mod, row0, y, xs, sz, ma, sgb, dexp, sng, w_bo, w_o, n2g, nfg, w1, w2):
    return pl.pallas_call(
        functools.partial(_sample_post_kernel, row0),
        out_shape=jax.ShapeDtypeStruct(x.shape, F32),
        compiler_params=pltpu.CompilerParams(vmem_limit_bytes=VMEM_LIMIT),
        name="sample_post",
    )(x, mod, y, xs, sz, ma, sgb, dexp, sng, w_bo, w_o, n2g, nfg, w1, w2)


def _transpose_cast_kernel(w_ref, o_ref):
    o_ref[...] = w_ref[0].T.astype(BF16)


def _transposed_section(w_t, layer, row0, rows):
    _, _, k = w_t.shape
    rb = min(rows, 512)
    assert rows % rb == 0
    return pl.pallas_call(
        _transpose_cast_kernel,
        grid=(rows // rb,),
        in_specs=[pl.BlockSpec((pl.Element(1), pl.Element(rb), pl.Element(k)),
                               lambda i: (layer, pl.multiple_of(row0 + i * rb, 8), 0))],
        out_specs=pl.BlockSpec((k, rb), lambda i: (0, i)),
        out_shape=jax.ShapeDtypeStruct((k, rows), BF16),
        compiler_params=pltpu.CompilerParams(
            dimension_semantics=("arbitrary",), vmem_limit_bytes=VMEM_LIMIT),
        name="w_in_section",
    )(w_t)


def _layer_weights(l, w_ada, b_ada, norm1_g, w_in, conv_a_w, w_a_out, conv_b_w, conv_b_b, dt_bias,
                   a_log, d_skip, ssm_norm_g, w_b_out, w_o, norm2_g, w_mlp1, w_mlp2):
    d = w_in.shape[1]
    d_inner = w_b_out.shape[1]
    d_xbc = conv_b_w.shape[2]
    n_heads = dt_bias.shape[1]
    o_z = 3 * d
    o_xbc = o_z + d_inner
    o_dt = o_xbc + d_xbc
    o_g = o_dt + n_heads
    row = lambda v: v.reshape(1, -1)
    lane_pad = lambda v: jnp.pad(v, ((0, 0), (0, LANES - v.shape[1])))
    w_in_t = jnp.transpose(w_in, (0, 2, 1))
    w_abc = _transposed_section(w_in_t, l, 0, o_z)
    w_z = _transposed_section(w_in_t, l, o_z, o_xbc - o_z)
    w_xbc = _transposed_section(w_in_t, l, o_xbc, o_dt - o_xbc)
    w_dt = lane_pad(_transposed_section(w_in_t, l, o_dt, o_g - o_dt))
    w_g = _transposed_section(w_in_t, l, o_g, 2 * d)
    return dict(
        w_ada=w_ada[l].astype(BF16), b_ada=row(b_ada[l]), n1g=row(norm1_g[l]),
        w_abc=w_abc, w_z=w_z, w_xbc=w_xbc, w_dt=w_dt, w_g=w_g,
        caw=conv_a_w[l], cbw=conv_b_w[l], cbb=row(conv_b_b[l]),
        dtb=lane_pad(row(dt_bias[l])), alog=lane_pad(row(a_log[l])),
        dexp=row(jnp.repeat(d_skip[l], HEAD_DIM)), sng=row(ssm_norm_g[l]),
        w_ao=w_a_out[l].astype(BF16), w_bo=w_b_out[l].astype(BF16), w_o=w_o[l].astype(BF16),
        n2g=row(norm2_g[l]), w1=w_mlp1[l].astype(BF16), w2=w_mlp2[l].astype(BF16))


def kernel(x_prompt, x_sample, c_prompt, c_sample, state_shortconv, state_ssm_conv, state_ssm, w_ada, b_ada, norm1_g, w_in, conv_a_w, w_a_out, conv_b_w, conv_b_b, dt_bias, a_log, d_skip, ssm_norm_g, w_b_out, w_o, norm2_g, w_mlp1, w_mlp2, norm_f_g):
    bp, lp, d = x_prompt.shape
    bs, ls, _ = x_sample.shape
    assert ls == 1
    depth = w_in.shape[0]
    d_inner = w_b_out.shape[1]
    n_heads = dt_bias.shape[1]
    assert n_heads <= LANES and d_inner == n_heads * HEAD_DIM

    e1 = (jnp.arange(LANES)[:, None] == (jnp.arange(d_inner)[None, :] // HEAD_DIM)).astype(BF16)
    e2 = jnp.concatenate([e1, e1], axis=0)
    nfg = norm_f_g.reshape(1, -1)

    assert depth == 1
    w = _layer_weights(0, w_ada, b_ada, norm1_g, w_in, conv_a_w, w_a_out, conv_b_w, conv_b_b,
                       dt_bias, a_log, d_skip, ssm_norm_g, w_b_out, w_o, norm2_g, w_mlp1, w_mlp2)
    mod = _modulation(jnp.concatenate([c_prompt, c_sample], axis=0), w["w_ada"], w["b_ada"])
    mod_p = mod[:bp].reshape(bp, 1, -1)

    xs0 = x_sample.reshape(bs, d)
    sa = jnp.transpose(state_shortconv[0], (1, 0, 2))
    sbc = jnp.transpose(state_ssm_conv[0], (1, 0, 2))
    ma, na_s, nbc_s, dtx, xs, bm, cm, dec, sz, sgb = _sample_pre(
        xs0, mod, bp, sa, sbc, w["n1g"], w["w_abc"], w["w_z"], w["w_xbc"], w["w_dt"], w["w_g"],
        w["caw"], w["cbw"], w["cbb"], w["dtb"], w["alog"], w["w_ao"], e2)

    x1, na, nbc, nssm = _prompt_mixer(
        x_prompt, mod_p, w["n1g"], w["w_abc"], w["w_z"], w["w_xbc"], w["w_dt"], w["w_g"], w["caw"],
        w["cbw"], w["cbb"], w["dtb"], w["alog"], w["dexp"], w["sng"], w["w_ao"], w["w_bo"],
        w["w_o"], e2, tq=256)
    yp, hn, yt = _prompt_mlp_with_state(
        x1.reshape(bp * lp, d), mod_p, w["n2g"], nfg, w["w1"], w["w2"],
        dec[:, :n_heads].reshape(-1), state_ssm[0], dtx.reshape(bs, n_heads, HEAD_DIM),
        bm.reshape(bs, N_GROUPS, D_STATE), cm.reshape(bs, N_GROUPS, D_STATE),
        tm=512, rows_per_seq=lp)

    ys = _sample_post(xs0, mod, bp, yt, xs, sz, ma, sgb, w["dexp"], w["sng"],
                      w["w_bo"], w["w_o"], w["n2g"], nfg, w["w1"], w["w2"])

    return (yp.reshape(bp, lp, d), ys.reshape(bs, ls, d), na[None], nbc[None], nssm[None],
            jnp.transpose(na_s, (1, 0, 2))[None], jnp.transpose(nbc_s, (1, 0, 2))[None], hn[None])
```

```python
import functools

import jax
import jax.numpy as jnp
from jax import lax
from jax.experimental import pallas as pl
from jax.experimental.pallas import tpu as pltpu

F32 = jnp.float32
BF16 = jnp.bfloat16

EPS = 1e-6
LOG2E = 1.4426950408889634
HEAD_DIM = 64
N_GROUPS = 4
D_STATE = 128
LANES = 128
SSD_CHUNK = 128
QUAD = 4
MLP_BLOCK = 1024
VMEM_LIMIT = 56 * 1024 * 1024


def _full_spec(shape):
    nd = len(shape)
    return pl.BlockSpec(shape, lambda *_: (0,) * nd, pipeline_mode=pl.Buffered(1))


def _bdot(a, b):
    return jnp.dot(a.astype(BF16), b, preferred_element_type=F32)


def _dot_nt(a, b):
    return lax.dot_general(a, b, (((1,), (1,)), ((), ())), preferred_element_type=F32)


def _rms(x, g):
    ms = jnp.mean(x * x, axis=-1, keepdims=True)
    return x * lax.rsqrt(ms + EPS) * g


def _silu(x):
    return x * jax.nn.sigmoid(x)


def _softplus(x):
    return jnp.maximum(x, 0.0) + jnp.log1p(jnp.exp(-jnp.abs(x)))


def _split2(v):
    hi = v.astype(BF16)
    lo = (v - hi.astype(F32)).astype(BF16)
    return hi, lo


def _split3(v):
    hi = v.astype(BF16)
    r = v - hi.astype(F32)
    mid = r.astype(BF16)
    lo = (r - mid.astype(F32)).astype(BF16)
    return hi, mid, lo


def _expand_heads(v, e2_ref):
    hi, lo = _split2(v)
    return jnp.dot(jnp.concatenate([hi, lo], axis=1), e2_ref[...], preferred_element_type=F32)


def _group_rmsnorm(y, g):
    d = y.shape[-1]
    w = d // N_GROUPS
    parts = []
    for k in range(N_GROUPS):
        yk = y[:, k * w:(k + 1) * w]
        ms = jnp.mean(yk * yk, axis=-1, keepdims=True)
        parts.append(yk * lax.rsqrt(ms + EPS))
    return jnp.concatenate(parts, axis=1) * g


def _mlp(x, sh2, sc2, g2, n2g, w1_ref, w2_ref, between=None):
    d_ff = w1_ref.shape[1]
    blk = MLP_BLOCK
    ub = (_rms(x, n2g) * (1.0 + sc2) + sh2).astype(BF16)
    acc = None
    for j in range(d_ff // blk):
        if between is not None:
            between(j)
        h = jnp.dot(ub, w1_ref[:, j * blk:(j + 1) * blk], preferred_element_type=F32)
        h = jnp.square(jnp.maximum(h, 0.0))
        p = _bdot(h, w2_ref[j * blk:(j + 1) * blk, :])
        acc = p if acc is None else acc + p
    if between is not None:
        between(d_ff // blk)
    return x + g2 * acc


def _mod_kernel(c_ref, w_ref, b_ref, o_ref):
    c = c_ref[...]
    o_ref[...] = _bdot(_silu(c), w_ref[...]) + b_ref[...]


def _modulation(c_all, w_ada, b_ada):
    n, d = c_all.shape
    dm = w_ada.shape[1]
    blk = 1024
    return pl.pallas_call(
        _mod_kernel,
        grid=(dm // blk,),
        in_specs=[pl.BlockSpec((n, d), lambda j: (0, 0)),
                  pl.BlockSpec((d, blk), lambda j: (0, j)),
                  pl.BlockSpec((1, blk), lambda j: (0, j))],
        out_specs=pl.BlockSpec((n, blk), lambda j: (0, j)),
        out_shape=jax.ShapeDtypeStruct((n, dm), F32),
        name="modulation",
    )(c_all, w_ada, b_ada)


def _tile_causal_conv(buf, cur, w_ref, pad):
    k_w = w_ref.shape[0]
    tq = cur.shape[0]
    buf[pad:pad + tq, :] = cur
    out = cur * w_ref[k_w - 1:k_w, :]
    for k in range(k_w - 1):
        lo = pad - (k_w - 1) + k
        out = out + buf[lo:lo + tq, :] * w_ref[k:k + 1, :]
    last = buf[pad + tq - (k_w - 1):pad + tq, :]
    buf[pad - (k_w - 1):pad, :] = last
    return out, last


def _ssd_chunk(xs_c, b_c, c_c, dt_c, a_row, d_row, ht_ref, e2_ref):
    q = xs_c.shape[0]
    n = D_STATE
    gw = xs_c.shape[1] // N_GROUPS
    qw = QUAD * HEAD_DIM
    row = lax.broadcasted_iota(jnp.int32, (q, q), 0)
    col = lax.broadcasted_iota(jnp.int32, (q, q), 1)
    causal = row >= col
    tri = jnp.where(causal, 1.0, 0.0).astype(BF16)
    lane_head = lax.broadcasted_iota(jnp.int32, (q, qw), 1) // HEAD_DIM
    head_mask = [jnp.where(lane_head == r, 1.0, 0.0).astype(BF16) for r in range(QUAD)]

    da = dt_c * (a_row * LOG2E)
    acs = sum(jnp.dot(tri, part, preferred_element_type=F32) for part in _split3(da))
    acs_t = acs.T
    dt_t = dt_c.T
    ea = jnp.exp2(acs)
    dd = jnp.exp2(acs[q - 1:q, :] - acs) * dt_c
    ea_x = _expand_heads(ea, e2_ref)
    dd_x = _expand_heads(dd, e2_ref)
    xd = (xs_c * dd_x).astype(BF16)
    bb = b_c.astype(BF16)
    cb_ = c_c.astype(BF16)

    y_parts = []
    for g in range(N_GROUPS):
        bg = bb[:, g * n:(g + 1) * n]
        cg = cb_[:, g * n:(g + 1) * n]
        cbm = _dot_nt(cg, bg)
        ht_g = ht_ref[:, g * gw:(g + 1) * gw]
        y_off = jnp.dot(cg, ht_g.astype(BF16), preferred_element_type=F32)
        for qd in range(gw // qw):
            lo = g * gw + qd * qw
            h0 = lo // HEAD_DIM
            ws = []
            for r in range(QUAD):
                h = h0 + r
                seg = acs[:, h:h + 1] - acs_t[h:h + 1, :]
                lm = jnp.exp2(jnp.where(causal, seg, -jnp.inf))
                ws.append((cbm * lm * dt_t[h:h + 1, :]).astype(BF16))
            wcat = jnp.concatenate(ws, axis=1)
            xq = xs_c[:, lo:lo + qw]
            xq_b = xq.astype(BF16)
            rhs = jnp.concatenate([xq_b * head_mask[r] for r in range(QUAD)], axis=0)
            y_diag = jnp.dot(wcat, rhs, preferred_element_type=F32)
            y_parts.append(y_diag + y_off[:, qd * qw:(qd + 1) * qw] * ea_x[:, lo:lo + qw]
                           + d_row[:, lo:lo + qw] * xq)
        bg_t = b_c[:, g * n:(g + 1) * n].T.astype(BF16)
        st = jnp.dot(bg_t, xd[:, g * gw:(g + 1) * gw], preferred_element_type=F32)
        ht_ref[:, g * gw:(g + 1) * gw] = ht_g * ea_x[q - 1:q, g * gw:(g + 1) * gw] + st
    return jnp.concatenate(y_parts, axis=1)


def _prompt_mixer_kernel(x_ref, mod_ref, n1g_ref, w_abc_ref, w_z_ref, w_xbc_ref, w_dt_ref, w_g_ref,
                         caw_ref, cbw_ref, cbb_ref, dtb_ref, alog_ref, dexp_ref, sng_ref,
                         w_ao_ref, w_bo_ref, w_o_ref, e2_ref,
                         x1_ref, na_ref, nbc_ref, nssm_ref,
                         cbuf, xbuf, ybuf, ht_ref):
    t = pl.program_id(1)
    nt = pl.num_programs(1)
    tq = x_ref.shape[1]
    d = x_ref.shape[2]
    d_inner = w_z_ref.shape[1]
    pad = 8

    @pl.when(t == 0)
    def _():
        cbuf[0:pad, :] = jnp.zeros((pad, cbuf.shape[1]), F32)
        xbuf[0:pad, :] = jnp.zeros((pad, xbuf.shape[1]), F32)
        ht_ref[...] = jnp.zeros(ht_ref.shape, F32)

    x = x_ref[0]
    sh1 = mod_ref[0, :, 0:d]
    sc1 = mod_ref[0, :, d:2 * d]
    g1 = mod_ref[0, :, 2 * d:3 * d]
    ub = (_rms(x, n1g_ref[...]) * (1.0 + sc1) + sh1).astype(BF16)

    def proj(w_ref, lo=None, hi=None):
        w = w_ref[...] if lo is None else w_ref[:, lo:hi]
        return jnp.dot(ub, w, preferred_element_type=F32)

    xbc = proj(w_xbc_ref)
    dt = _softplus(proj(w_dt_ref) + dtb_ref[...])
    a_row = -jnp.exp(alog_ref[...])
    xc, new_bc = _tile_causal_conv(xbuf, xbc, cbw_ref, pad)
    cgate = proj(w_abc_ref, d, 2 * d)
    hval = proj(w_abc_ref, 2 * d, 3 * d)
    xc = _silu(xc + cbb_ref[...])
    bgate = proj(w_abc_ref, 0, d)
    conv, new_a = _tile_causal_conv(cbuf, cgate * hval, caw_ref, pad)

    gn = N_GROUPS * D_STATE
    fillers = [lambda: proj(w_z_ref), lambda: _bdot(bgate * conv, w_ao_ref[...])]
    filled = []
    for c in range(tq // SSD_CHUNK):
        s = slice(c * SSD_CHUNK, (c + 1) * SSD_CHUNK)
        ybuf[s, :] = _ssd_chunk(xc[s, 0:d_inner], xc[s, d_inner:d_inner + gn],
                                xc[s, d_inner + gn:d_inner + 2 * gn], dt[s, :], a_row,
                                dexp_ref[...], ht_ref, e2_ref)
        if c < len(fillers):
            filled.append(fillers[c]())
    filled += [f() for f in fillers[len(filled):]]
    z, y_a = filled

    merged = jax.nn.sigmoid(proj(w_g_ref, 0, d)) * y_a
    y_b = _bdot(_group_rmsnorm(ybuf[...] * _silu(z), sng_ref[...]), w_bo_ref[...])
    merged = merged + jax.nn.sigmoid(proj(w_g_ref, d, 2 * d)) * y_b
    x1_ref[0] = x + g1 * _bdot(merged, w_o_ref[...])

    @pl.when(t == nt - 1)
    def _():
        na_ref[0] = new_a
        nbc_ref[0] = new_bc
        h = ht_ref[...].T
        nssm_ref[0] = h.reshape(nssm_ref.shape[1:])


def _prompt_mixer(x, mod_p, n1g, w_abc, w_z, w_xbc, w_dt, w_g, caw, cbw, cbb, dtb, alog, dexp, sng,
                  w_ao, w_bo, w_o, e2, *, tq):
    b, l, d = x.shape
    d_inner = w_z.shape[1]
    d_xbc = w_xbc.shape[1]
    n_heads = d_inner // HEAD_DIM
    consts = (n1g, w_abc, w_z, w_xbc, w_dt, w_g, caw, cbw, cbb, dtb, alog, dexp, sng, w_ao, w_bo, w_o, e2)
    return pl.pallas_call(
        _prompt_mixer_kernel,
        grid=(b, l // tq),
        in_specs=[pl.BlockSpec((1, tq, d), lambda i, t: (i, t, 0)),
                  pl.BlockSpec((1, 1, mod_p.shape[2]), lambda i, t: (i, 0, 0))]
                 + [_full_spec(c.shape) for c in consts],
        out_specs=[pl.BlockSpec((1, tq, d), lambda i, t: (i, t, 0)),
                   pl.BlockSpec((1, caw.shape[0] - 1, d), lambda i, t: (i, 0, 0)),
                   pl.BlockSpec((1, cbw.shape[0] - 1, d_xbc), lambda i, t: (i, 0, 0)),
                   pl.BlockSpec((1, n_heads, HEAD_DIM, D_STATE), lambda i, t: (i, 0, 0, 0))],
        out_shape=[jax.ShapeDtypeStruct((b, l, d), F32),
                   jax.ShapeDtypeStruct((b, caw.shape[0] - 1, d), F32),
                   jax.ShapeDtypeStruct((b, cbw.shape[0] - 1, d_xbc), F32),
                   jax.ShapeDtypeStruct((b, n_heads, HEAD_DIM, D_STATE), F32)],
        scratch_shapes=[pltpu.VMEM((tq + 8, d), F32),
                        pltpu.VMEM((tq + 8, d_xbc), F32),
                        pltpu.VMEM((tq, d_inner), F32),
                        pltpu.VMEM((D_STATE, d_inner), F32)],
        compiler_params=pltpu.CompilerParams(
            dimension_semantics=("arbitrary", "arbitrary"), vmem_limit_bytes=VMEM_LIMIT),
        name="prompt_mixer",
    )(x, mod_p, *consts)


def _state_update_groups(groups, seq0, dec_ref, h0_ref, dtx_ref, bm_ref, cm_ref, hn_ref, yt_ref):
    bb, n_heads, p, n = h0_ref.shape
    per = n_heads // N_GROUPS
    n_seq = yt_ref.shape[1]
    lane = lax.broadcasted_iota(jnp.int32, (n, n_seq), 1)
    x_t = [dtx_ref[j].T for j in range(bb)]
    c_t = [cm_ref[j].T for j in range(bb)]
    lhs_rows, rhs_rows = [], []
    for t, g in enumerate(groups):
        lhs, rhs = [], []
        for j in range(bb):
            brow = bm_ref[j, g:g + 1, :]
            hs = []
            for r in range(per):
                h = g * per + r
                hn = dec_ref[(seq0 + j) * n_heads + h] * h0_ref[j, h] + x_t[j][:, h:h + 1] * brow
                hn_ref[j, h] = hn
                hs.append(hn.astype(BF16))
            lhs.append(jnp.concatenate(hs, axis=0))
            rhs.append(jnp.where(lane == seq0 + j, c_t[j][:, g:g + 1], 0.0).astype(BF16))
        lhs_rows.append(jnp.concatenate(lhs, axis=1))
        col = jnp.concatenate(rhs, axis=0)
        zero = jnp.zeros_like(col)
        rhs_rows.append(jnp.concatenate([col if u == t else zero for u in range(len(groups))], axis=1))
    out = jnp.dot(jnp.concatenate(lhs_rows, axis=1), jnp.concatenate(rhs_rows, axis=0),
                  preferred_element_type=F32)
    for t, g in enumerate(groups):
        rows = slice(g * per * p, (g + 1) * per * p)
        yt_ref[rows, :] += out[:, t * n_seq:(t + 1) * n_seq]


def _prompt_mlp_kernel(dec_ref, x_ref, mod_ref, n2g_ref, nfg_ref, w1_ref, w2_ref,
                       h0_ref, dtx_ref, bm_ref, cm_ref, o_ref, hn_ref, yt_ref):
    i = pl.program_id(0)
    d = x_ref.shape[1]
    bb = h0_ref.shape[0]
    n_blk = w1_ref.shape[1] // MLP_BLOCK
    sh2 = mod_ref[0, :, 3 * d:4 * d]
    sc2 = mod_ref[0, :, 4 * d:5 * d]
    g2 = mod_ref[0, :, 5 * d:6 * d]

    @pl.when(i == 0)
    def _():
        yt_ref[...] = jnp.zeros(yt_ref.shape, F32)

    def between(blk):
        if blk == n_blk - 2:
            for g in range(N_GROUPS):
                _state_update_groups((g,), i * bb, dec_ref, h0_ref, dtx_ref, bm_ref, cm_ref, hn_ref,
                                     yt_ref)

    x2 = _mlp(x_ref[...], sh2, sc2, g2, n2g_ref[...], w1_ref, w2_ref, between)
    o_ref[...] = _rms(x2, nfg_ref[...])


def _prompt_mlp_with_state(x1, mod_p, n2g, nfg, w1, w2, dec_flat, h0, dtx3, bm3, cm3, *, tm,
                           rows_per_seq):
    t, d = x1.shape
    per = rows_per_seq // tm
    steps = t // tm
    n, n_heads, p, ns = h0.shape
    assert n % steps == 0
    bb = n // steps

    def full(shape):
        nd = len(shape)
        return pl.BlockSpec(shape, lambda i, s: (0,) * nd, pipeline_mode=pl.Buffered(1))

    return pl.pallas_call(
        _prompt_mlp_kernel,
        grid_spec=pltpu.PrefetchScalarGridSpec(
            num_scalar_prefetch=1,
            grid=(steps,),
            in_specs=[pl.BlockSpec((tm, d), lambda i, s: (i, 0)),
                      pl.BlockSpec((1, 1, mod_p.shape[2]), lambda i, s: (i // per, 0, 0)),
                      full(n2g.shape), full(nfg.shape), full(w1.shape), full(w2.shape),
                      pl.BlockSpec((bb, n_heads, p, ns), lambda i, s: (i, 0, 0, 0)),
                      pl.BlockSpec((bb, n_heads, p), lambda i, s: (i, 0, 0)),
                      pl.BlockSpec((bb, N_GROUPS, ns), lambda i, s: (i, 0, 0)),
                      pl.BlockSpec((bb, N_GROUPS, ns), lambda i, s: (i, 0, 0))],
            out_specs=[pl.BlockSpec((tm, d), lambda i, s: (i, 0)),
                       pl.BlockSpec((bb, n_heads, p, ns), lambda i, s: (i, 0, 0, 0)),
                       pl.BlockSpec((n_heads * p, n), lambda i, s: (0, 0))]),
        out_shape=[jax.ShapeDtypeStruct((t, d), F32),
                   jax.ShapeDtypeStruct(h0.shape, F32),
                   jax.ShapeDtypeStruct((n_heads * p, n), F32)],
        compiler_params=pltpu.CompilerParams(
            dimension_semantics=("arbitrary",), vmem_limit_bytes=VMEM_LIMIT),
        name="prompt_mlp",
    )(dec_flat, x1, mod_p, n2g, nfg, w1, w2, h0, dtx3, bm3, cm3)


def _sample_pre_kernel(row0, x_ref, mod_ref, sa_ref, sbc_ref, n1g_ref, w_abc_ref, w_z_ref, w_xbc_ref,
                       w_dt_ref, w_g_ref, caw_ref, cbw_ref, cbb_ref, dtb_ref, alog_ref, w_ao_ref,
                       e2_ref,
                       ma_ref, na_ref, nbc_ref, dtx_ref, xs_ref, bm_ref, cm_ref, dec_ref, sz_ref,
                       sgb_ref):
    d = x_ref.shape[1]
    d_inner = w_z_ref.shape[1]
    ka = caw_ref.shape[0]
    kb = cbw_ref.shape[0]
    x = x_ref[...]
    rows = slice(row0, row0 + x.shape[0])
    sh1 = mod_ref[rows, 0:d]
    sc1 = mod_ref[rows, d:2 * d]
    ub = (_rms(x, n1g_ref[...]) * (1.0 + sc1) + sh1).astype(BF16)

    bgate = jnp.dot(ub, w_abc_ref[:, 0:d], preferred_element_type=F32)
    cgate = jnp.dot(ub, w_abc_ref[:, d:2 * d], preferred_element_type=F32)
    hval = jnp.dot(ub, w_abc_ref[:, 2 * d:3 * d], preferred_element_type=F32)
    ch = cgate * hval
    conv = ch * caw_ref[ka - 1:ka, :]
    for k in range(ka - 1):
        conv = conv + sa_ref[k] * caw_ref[k:k + 1, :]
    for k in range(ka - 2):
        na_ref[k] = sa_ref[k + 1]
    na_ref[ka - 2] = ch
    y_a = _bdot(bgate * conv, w_ao_ref[...])
    ma_ref[...] = jax.nn.sigmoid(jnp.dot(ub, w_g_ref[:, 0:d], preferred_element_type=F32)) * y_a
    sgb_ref[...] = jax.nn.sigmoid(jnp.dot(ub, w_g_ref[:, d:2 * d], preferred_element_type=F32))

    xbc = jnp.dot(ub, w_xbc_ref[...], preferred_element_type=F32)
    xc = xbc * cbw_ref[kb - 1:kb, :]
    for k in range(kb - 1):
        xc = xc + sbc_ref[k] * cbw_ref[k:k + 1, :]
    xc = _silu(xc + cbb_ref[...])
    for k in range(kb - 2):
        nbc_ref[k] = sbc_ref[k + 1]
    nbc_ref[kb - 2] = xbc

    gn = N_GROUPS * D_STATE
    xs = xc[:, 0:d_inner]
    dt = _softplus(jnp.dot(ub, w_dt_ref[...], preferred_element_type=F32) + dtb_ref[...])
    dec_ref[...] = jnp.exp(dt * (-jnp.exp(alog_ref[...])))
    hi, mid, lo = _split3(dt)
    e3 = e2_ref[0:LANES, :]
    dt_x = (jnp.dot(hi, e3, preferred_element_type=F32) + jnp.dot(mid, e3, preferred_element_type=F32)
            + jnp.dot(lo, e3, preferred_element_type=F32))
    dtx_ref[...] = xs * dt_x
    xs_ref[...] = xs
    bm_ref[...] = xc[:, d_inner:d_inner + gn]
    cm_ref[...] = xc[:, d_inner + gn:d_inner + 2 * gn]
    sz_ref[...] = _silu(jnp.dot(ub, w_z_ref[...], preferred_element_type=F32))


def _sample_pre(x, mod, row0, sa, sbc, n1g, w_abc, w_z, w_xbc, w_dt, w_g, caw, cbw, cbb, dtb, alog, w_ao, e2):
    n, d = x.shape
    d_inner = w_z.shape[1]
    gn = N_GROUPS * D_STATE
    shapes = [(n, d), sa.shape, sbc.shape, (n, d_inner), (n, d_inner), (n, gn), (n, gn), (n, LANES),
              (n, d_inner), (n, d)]
    return pl.pallas_call(
        functools.partial(_sample_pre_kernel, row0),
        out_shape=[jax.ShapeDtypeStruct(s, F32) for s in shapes],
        compiler_params=pltpu.CompilerParams(vmem_limit_bytes=VMEM_LIMIT),
        name="sample_pre",
    )(x, mod, sa, sbc, n1g, w_abc, w_z, w_xbc, w_dt, w_g, caw, cbw, cbb, dtb, alog, w_ao, e2)


def _sample_post_kernel(row0, x_ref, mod_ref, yt_ref, xs_ref, sz_ref, ma_ref, sgb_ref, dexp_ref, sng_ref,
                        w_bo_ref, w_o_ref, n2g_ref, nfg_ref, w1_ref, w2_ref, o_ref):
    d = x_ref.shape[1]
    rows = slice(row0, row0 + x_ref.shape[0])
    g1 = mod_ref[rows, 2 * d:3 * d]
    sh2 = mod_ref[rows, 3 * d:4 * d]
    sc2 = mod_ref[rows, 4 * d:5 * d]
    g2 = mod_ref[rows, 5 * d:6 * d]
    ys = (yt_ref[...].T + dexp_ref[...] * xs_ref[...]) * sz_ref[...]
    y_b = _bdot(_group_rmsnorm(ys, sng_ref[...]), w_bo_ref[...])
    merged = ma_ref[...] + sgb_ref[...] * y_b
    x1 = x_ref[...] + g1 * _bdot(merged, w_o_ref[...])
    x2 = _mlp(x1, sh2, sc2, g2, n2g_ref[...], w1_ref, w2_ref)
    o_ref[...] = _rms(x2, nfg_ref[...])


def _sample_post(x, mod, row0, y, xs, sz, ma, sgb, dexp, sng, w_bo, w_o, n2g, nfg, w1, w2):
    return pl.pallas_call(
        functools.partial(_sample_post_kernel, row0),
        out_shape=jax.ShapeDtypeStruct(x.shape, F32),
        compiler_params=pltpu.CompilerParams(vmem_limit_bytes=VMEM_LIMIT),
        name="sample_post",
    )(x, mod, y, xs, sz, ma, sgb, dexp, sng, w_bo, w_o, n2g, nfg, w1, w2)


def _transpose_cast_kernel(w_ref, o_ref):
    o_ref[...] = w_ref[0].T.astype(BF16)


def _transposed_section(w_t, layer, row0, rows):
    _, _, k = w_t.shape
    rb = min(rows, 512)
    assert rows % rb == 0
    return pl.pallas_call(
        _transpose_cast_kernel,
        grid=(rows // rb,),
        in_specs=[pl.BlockSpec((pl.Element(1), pl.Element(rb), pl.Element(k)),
                               lambda i: (layer, pl.multiple_of(row0 + i * rb, 8), 0))],
        out_specs=pl.BlockSpec((k, rb), lambda i: (0, i)),
        out_shape=jax.ShapeDtypeStruct((k, rows), BF16),
        compiler_params=pltpu.CompilerParams(
            dimension_semantics=("arbitrary",), vmem_limit_bytes=VMEM_LIMIT),
        name="w_in_section",
    )(w_t)


def _layer_weights(l, w_ada, b_ada, norm1_g, w_in, conv_a_w, w_a_out, conv_b_w, conv_b_b, dt_bias,
                   a_log, d_skip, ssm_norm_g, w_b_out, w_o, norm2_g, w_mlp1, w_mlp2):
    d = w_in.shape[1]
    d_inner = w_b_out.shape[1]
    d_xbc = conv_b_w.shape[2]
    n_heads = dt_bias.shape[1]
    o_z = 3 * d
    o_xbc = o_z + d_inner
    o_dt = o_xbc + d_xbc
    o_g = o_dt + n_heads
    row = lambda v: v.reshape(1, -1)
    lane_pad = lambda v: jnp.pad(v, ((0, 0), (0, LANES - v.shape[1])))
    w_in_t = jnp.transpose(w_in, (0, 2, 1))
    w_abc = _transposed_section(w_in_t, l, 0, o_z)
    w_z = _transposed_section(w_in_t, l, o_z, o_xbc - o_z)
    w_xbc = _transposed_section(w_in_t, l, o_xbc, o_dt - o_xbc)
    w_dt = lane_pad(_transposed_section(w_in_t, l, o_dt, o_g - o_dt))
    w_g = _transposed_section(w_in_t, l, o_g, 2 * d)
    return dict(
        w_ada=w_ada[l].astype(BF16), b_ada=row(b_ada[l]), n1g=row(norm1_g[l]),
        w_abc=w_abc, w_z=w_z, w_xbc=w_xbc, w_dt=w_dt, w_g=w_g,
        caw=conv_a_w[l], cbw=conv_b_w[l], cbb=row(conv_b_b[l]),
        dtb=lane_pad(row(dt_bias[l])), alog=lane_pad(row(a_log[l])),
        dexp=row(jnp.repeat(d_skip[l], HEAD_DIM)), sng=row(ssm_norm_g[l]),
        w_ao=w_a_out[l].astype(BF16), w_bo=w_b_out[l].astype(BF16), w_o=w_o[l].astype(BF16),
        n2g=row(norm2_g[l]), w1=w_mlp1[l].astype(BF16), w2=w_mlp2[l].astype(BF16))


def kernel(x_prompt, x_sample, c_prompt, c_sample, state_shortconv, state_ssm_conv, state_ssm, w_ada, b_ada, norm1_g, w_in, conv_a_w, w_a_out, conv_b_w, conv_b_b, dt_bias, a_log, d_skip, ssm_norm_g, w_b_out, w_o, norm2_g, w_mlp1, w_mlp2, norm_f_g):
    bp, lp, d = x_prompt.shape
    bs, ls, _ = x_sample.shape
    assert ls == 1
    depth = w_in.shape[0]
    d_inner = w_b_out.shape[1]
    n_heads = dt_bias.shape[1]
    assert n_heads <= LANES and d_inner == n_heads * HEAD_DIM

    e1 = (jnp.arange(LANES)[:, None] == (jnp.arange(d_inner)[None, :] // HEAD_DIM)).astype(BF16)
    e2 = jnp.concatenate([e1, e1], axis=0)
    nfg = norm_f_g.reshape(1, -1)

    assert depth == 1
    w = _layer_weights(0, w_ada, b_ada, norm1_g, w_in, conv_a_w, w_a_out, conv_b_w, conv_b_b,
                       dt_bias, a_log, d_skip, ssm_norm_g, w_b_out, w_o, norm2_g, w_mlp1, w_mlp2)
    mod = _modulation(jnp.concatenate([c_prompt, c_sample], axis=0), w["w_ada"], w["b_ada"])
    mod_p = mod[:bp].reshape(bp, 1, -1)

    xs0 = x_sample.reshape(bs, d)
    sa = jnp.transpose(state_shortconv[0], (1, 0, 2))
    sbc = jnp.transpose(state_ssm_conv[0], (1, 0, 2))
    ma, na_s, nbc_s, dtx, xs, bm, cm, dec, sz, sgb = _sample_pre(
        xs0, mod, bp, sa, sbc, w["n1g"], w["w_abc"], w["w_z"], w["w_xbc"], w["w_dt"], w["w_g"],
        w["caw"], w["cbw"], w["cbb"], w["dtb"], w["alog"], w["w_ao"], e2)

    x1, na, nbc, nssm = _prompt_mixer(
        x_prompt, mod_p, w["n1g"], w["w_abc"], w["w_z"], w["w_xbc"], w["w_dt"], w["w_g"], w["caw"],
        w["cbw"], w["cbb"], w["dtb"], w["alog"], w["dexp"], w["sng"], w["w_ao"], w["w_bo"],
        w["w_o"], e2, tq=256)
    yp, hn, yt = _prompt_mlp_with_state(
        x1.reshape(bp * lp, d), mod_p, w["n2g"], nfg, w["w1"], w["w2"],
        dec[:, :n_heads].reshape(-1), state_ssm[0], dtx.reshape(bs, n_heads, HEAD_DIM),
        bm.reshape(bs, N_GROUPS, D_STATE), cm.reshape(bs, N_GROUPS, D_STATE),
        tm=512, rows_per_seq=lp)

    ys = _sample_post(xs0, mod, bp, yt, xs, sz, ma, sgb, w["dexp"], w["sng"],
                      w["w_bo"], w["w_o"], w["n2g"], nfg, w["w1"], w["w2"])

    return (yp.reshape(bp, lp, d), ys.reshape(bs, ls, d), na[None], nbc[None], nssm[None],
            jnp.transpose(na_s, (1, 0, 2))[None], jnp.transpose(nbc_s, (1, 0, 2))[None], hn[None])
```

```python
import functools

import jax
import jax.numpy as jnp
from jax import lax
from jax.experimental import pallas as pl
from jax.experimental.pallas import tpu as pltpu

F32 = jnp.float32
BF16 = jnp.bfloat16

EPS = 1e-6
LOG2E = 1.4426950408889634
HEAD_DIM = 64
N_GROUPS = 4
D_STATE = 128
LANES = 128
SSD_CHUNK = 128
QUAD = 4
MLP_BLOCK = 1024
VMEM_LIMIT = 56 * 1024 * 1024


def _full_spec(shape):
    nd = len(shape)
    return pl.BlockSpec(shape, lambda *_: (0,) * nd, pipeline_mode=pl.Buffered(1))


def _bdot(a, b):
    return jnp.dot(a.astype(BF16), b, preferred_element_type=F32)


def _dot_nt(a, b):
    return lax.dot_general(a, b, (((1,), (1,)), ((), ())), preferred_element_type=F32)


def _rms(x, g):
    ms = jnp.mean(x * x, axis=-1, keepdims=True)
    return x * lax.rsqrt(ms + EPS) * g


def _silu(x):
    return x * jax.nn.sigmoid(x)


def _softplus(x):
    return jnp.maximum(x, 0.0) + jnp.log1p(jnp.exp(-jnp.abs(x)))


def _split2(v):
    hi = v.astype(BF16)
    lo = (v - hi.astype(F32)).astype(BF16)
    return hi, lo


def _split3(v):
    hi = v.astype(BF16)
    r = v - hi.astype(F32)
    mid = r.astype(BF16)
    lo = (r - mid.astype(F32)).astype(BF16)
    return hi, mid, lo


def _expand_heads(v, e2_ref):
    hi, lo = _split2(v)
    return jnp.dot(jnp.concatenate([hi, lo], axis=1), e2_ref[...], preferred_element_type=F32)


def _group_rmsnorm(y, g):
    d = y.shape[-1]
    w = d // N_GROUPS
    parts = []
    for k in range(N_GROUPS):
        yk = y[:, k * w:(k + 1) * w]
        ms = jnp.mean(yk * yk, axis=-1, keepdims=True)
        parts.append(yk * lax.rsqrt(ms + EPS))
    return jnp.concatenate(parts, axis=1) * g


def _mlp(x, sh2, sc2, g2, n2g, w1_ref, w2_ref, between=None):
    d_ff = w1_ref.shape[1]
    blk = MLP_BLOCK
    ub = (_rms(x, n2g) * (1.0 + sc2) + sh2).astype(BF16)
    acc = None
    for j in range(d_ff // blk):
        if between is not None:
            between(j)
        h = jnp.dot(ub, w1_ref[:, j * blk:(j + 1) * blk], preferred_element_type=F32)
        h = jnp.square(jnp.maximum(h, 0.0))
        p = _bdot(h, w2_ref[j * blk:(j + 1) * blk, :])
        acc = p if acc is None else acc + p
    if between is not None:
        between(d_ff // blk)
    return x + g2 * acc


def _mod_kernel(c_ref, w_ref, b_ref, o_ref):
    c = c_ref[...]
    o_ref[...] = _bdot(_silu(c), w_ref[...].astype(BF16)) + b_ref[...]


def _modulation(c_all, w_ada, b_ada):
    n, d = c_all.shape
    dm = w_ada.shape[1]
    blk = 1024
    return pl.pallas_call(
        _mod_kernel,
        grid=(dm // blk,),
        in_specs=[pl.BlockSpec((n, d), lambda j: (0, 0)),
                  pl.BlockSpec((d, blk), lambda j: (0, j)),
                  pl.BlockSpec((1, blk), lambda j: (0, j))],
        out_specs=pl.BlockSpec((n, blk), lambda j: (0, j)),
        out_shape=jax.ShapeDtypeStruct((n, dm), F32),
        name="modulation",
    )(c_all, w_ada, b_ada)


def _tile_causal_conv(buf, cur, w_ref, pad):
    k_w = w_ref.shape[0]
    tq = cur.shape[0]
    buf[pad:pad + tq, :] = cur
    out = cur * w_ref[k_w - 1:k_w, :]
    for k in range(k_w - 1):
        lo = pad - (k_w - 1) + k
        out = out + buf[lo:lo + tq, :] * w_ref[k:k + 1, :]
    last = buf[pad + tq - (k_w - 1):pad + tq, :]
    buf[pad - (k_w - 1):pad, :] = last
    return out, last


def _ssd_chunk(xs_c, b_c, c_c, dt_c, a_row, d_row, ht_ref, e2_ref):
    q = xs_c.shape[0]
    n = D_STATE
    gw = xs_c.shape[1] // N_GROUPS
    qw = QUAD * HEAD_DIM
    row = lax.broadcasted_iota(jnp.int32, (q, q), 0)
    col = lax.broadcasted_iota(jnp.int32, (q, q), 1)
    causal = row >= col
    tri = jnp.where(causal, 1.0, 0.0).astype(BF16)
    lane_head = lax.broadcasted_iota(jnp.int32, (q, qw), 1) // HEAD_DIM
    head_mask = [jnp.where(lane_head == r, 1.0, 0.0).astype(BF16) for r in range(QUAD)]

    da = dt_c * (a_row * LOG2E)
    acs = sum(jnp.dot(tri, part, preferred_element_type=F32) for part in _split3(da))
    acs_t = acs.T
    dt_t = dt_c.T
    ea = jnp.exp2(acs)
    dd = jnp.exp2(acs[q - 1:q, :] - acs) * dt_c
    ea_x = _expand_heads(ea, e2_ref)
    dd_x = _expand_heads(dd, e2_ref)
    xd = (xs_c * dd_x).astype(BF16)
    bb = b_c.astype(BF16)
    cb_ = c_c.astype(BF16)

    y_parts = []
    for g in range(N_GROUPS):
        bg = bb[:, g * n:(g + 1) * n]
        cg = cb_[:, g * n:(g + 1) * n]
        cbm = _dot_nt(cg, bg)
        ht_g = ht_ref[:, g * gw:(g + 1) * gw]
        y_off = jnp.dot(cg, ht_g.astype(BF16), preferred_element_type=F32)
        for qd in range(gw // qw):
            lo = g * gw + qd * qw
            h0 = lo // HEAD_DIM
            ws = []
            for r in range(QUAD):
                h = h0 + r
                seg = acs[:, h:h + 1] - acs_t[h:h + 1, :]
                lm = jnp.exp2(jnp.where(causal, seg, -jnp.inf))
                ws.append((cbm * lm * dt_t[h:h + 1, :]).astype(BF16))
            wcat = jnp.concatenate(ws, axis=1)
            xq = xs_c[:, lo:lo + qw]
            xq_b = xq.astype(BF16)
            rhs = jnp.concatenate([xq_b * head_mask[r] for r in range(QUAD)], axis=0)
            y_diag = jnp.dot(wcat, rhs, preferred_element_type=F32)
            y_parts.append(y_diag + y_off[:, qd * qw:(qd + 1) * qw] * ea_x[:, lo:lo + qw]
                           + d_row[:, lo:lo + qw] * xq)
        bg_t = b_c[:, g * n:(g + 1) * n].T.astype(BF16)
        st = jnp.dot(bg_t, xd[:, g * gw:(g + 1) * gw], preferred_element_type=F32)
        ht_ref[:, g * gw:(g + 1) * gw] = ht_g * ea_x[q - 1:q, g * gw:(g + 1) * gw] + st
    return jnp.concatenate(y_parts, axis=1)


def _prompt_mixer_kernel(x_ref, mod_ref, n1g_ref, w_abc_ref, w_z_ref, w_xbc_ref, w_dt_ref, w_g_ref,
                         caw_ref, cbw_ref, cbb_ref, dtb_ref, alog_ref, dexp_ref, sng_ref,
                         w_ao_ref, w_bo_ref, w_o_ref, e2_ref, w1f_ref, w2f_ref,
                         x1_ref, na_ref, nbc_ref, nssm_ref, w1_ref, w2_ref,
                         cbuf, xbuf, ybuf, ht_ref):
    w1_ref[...] = w1f_ref[...].astype(BF16)
    w2_ref[...] = w2f_ref[...].astype(BF16)

    t = pl.program_id(1)
    nt = pl.num_programs(1)
    tq = x_ref.shape[1]
    d = x_ref.shape[2]
    d_inner = w_z_ref.shape[1]
    pad = 8

    @pl.when(t == 0)
    def _():
        cbuf[0:pad, :] = jnp.zeros((pad, cbuf.shape[1]), F32)
        xbuf[0:pad, :] = jnp.zeros((pad, xbuf.shape[1]), F32)
        ht_ref[...] = jnp.zeros(ht_ref.shape, F32)

    x = x_ref[0]
    sh1 = mod_ref[0, :, 0:d]
    sc1 = mod_ref[0, :, d:2 * d]
    g1 = mod_ref[0, :, 2 * d:3 * d]
    ub = (_rms(x, n1g_ref[...]) * (1.0 + sc1) + sh1).astype(BF16)

    def proj(w_ref, lo=None, hi=None):
        w = w_ref[...] if lo is None else w_ref[:, lo:hi]
        return jnp.dot(ub, w, preferred_element_type=F32)

    xbc = proj(w_xbc_ref)
    dt = _softplus(proj(w_dt_ref) + dtb_ref[...])
    a_row = -jnp.exp(alog_ref[...])
    xc, new_bc = _tile_causal_conv(xbuf, xbc, cbw_ref, pad)
    cgate = proj(w_abc_ref, d, 2 * d)
    hval = proj(w_abc_ref, 2 * d, 3 * d)
    xc = _silu(xc + cbb_ref[...])
    bgate = proj(w_abc_ref, 0, d)
    conv, new_a = _tile_causal_conv(cbuf, cgate * hval, caw_ref, pad)

    gn = N_GROUPS * D_STATE
    fillers = [lambda: proj(w_z_ref), lambda: _bdot(bgate * conv, w_ao_ref[...])]
    filled = []
    for c in range(tq // SSD_CHUNK):
        s = slice(c * SSD_CHUNK, (c + 1) * SSD_CHUNK)
        ybuf[s, :] = _ssd_chunk(xc[s, 0:d_inner], xc[s, d_inner:d_inner + gn],
                                xc[s, d_inner + gn:d_inner + 2 * gn], dt[s, :], a_row,
                                dexp_ref[...], ht_ref, e2_ref)
        if c < len(fillers):
            filled.append(fillers[c]())
    filled += [f() for f in fillers[len(filled):]]
    z, y_a = filled

    merged = jax.nn.sigmoid(proj(w_g_ref, 0, d)) * y_a
    y_b = _bdot(_group_rmsnorm(ybuf[...] * _silu(z), sng_ref[...]), w_bo_ref[...])
    merged = merged + jax.nn.sigmoid(proj(w_g_ref, d, 2 * d)) * y_b
    x1_ref[0] = x + g1 * _bdot(merged, w_o_ref[...])

    @pl.when(t == nt - 1)
    def _():
        na_ref[0] = new_a
        nbc_ref[0] = new_bc
        h = ht_ref[...].T
        nssm_ref[0] = h.reshape(nssm_ref.shape[1:])


def _prompt_mixer(x, mod_p, n1g, w_abc, w_z, w_xbc, w_dt, w_g, caw, cbw, cbb, dtb, alog, dexp, sng,
                  w_ao, w_bo, w_o, e2, w1_f32, w2_f32, *, tq):
    b, l, d = x.shape
    nt = l // tq
    steps = b * nt
    r1 = w1_f32.shape[0] // steps
    r2 = w2_f32.shape[0] // steps
    assert r1 * steps == w1_f32.shape[0] and r2 * steps == w2_f32.shape[0] and r1 % 16 == 0
    d_inner = w_z.shape[1]
    d_xbc = w_xbc.shape[1]
    n_heads = d_inner // HEAD_DIM
    consts = (n1g, w_abc, w_z, w_xbc, w_dt, w_g, caw, cbw, cbb, dtb, alog, dexp, sng, w_ao, w_bo, w_o, e2)
    return pl.pallas_call(
        _prompt_mixer_kernel,
        grid=(b, l // tq),
        in_specs=[pl.BlockSpec((1, tq, d), lambda i, t: (i, t, 0)),
                  pl.BlockSpec((1, 1, mod_p.shape[2]), lambda i, t: (i, 0, 0))]
                 + [_full_spec(c.shape) for c in consts]
                 + [pl.BlockSpec((r1, w1_f32.shape[1]), lambda i, t: (i * nt + t, 0)),
                    pl.BlockSpec((r2, w2_f32.shape[1]), lambda i, t: (i * nt + t, 0))],
        out_specs=[pl.BlockSpec((1, tq, d), lambda i, t: (i, t, 0)),
                   pl.BlockSpec((1, caw.shape[0] - 1, d), lambda i, t: (i, 0, 0)),
                   pl.BlockSpec((1, cbw.shape[0] - 1, d_xbc), lambda i, t: (i, 0, 0)),
                   pl.BlockSpec((1, n_heads, HEAD_DIM, D_STATE), lambda i, t: (i, 0, 0, 0)),
                   pl.BlockSpec((r1, w1_f32.shape[1]), lambda i, t: (i * nt + t, 0)),
                   pl.BlockSpec((r2, w2_f32.shape[1]), lambda i, t: (i * nt + t, 0))],
        out_shape=[jax.ShapeDtypeStruct((b, l, d), F32),
                   jax.ShapeDtypeStruct((b, caw.shape[0] - 1, d), F32),
                   jax.ShapeDtypeStruct((b, cbw.shape[0] - 1, d_xbc), F32),
                   jax.ShapeDtypeStruct((b, n_heads, HEAD_DIM, D_STATE), F32),
                   jax.ShapeDtypeStruct(w1_f32.shape, BF16),
                   jax.ShapeDtypeStruct(w2_f32.shape, BF16)],
        scratch_shapes=[pltpu.VMEM((tq + 8, d), F32),
                        pltpu.VMEM((tq + 8, d_xbc), F32),
                        pltpu.VMEM((tq, d_inner), F32),
                        pltpu.VMEM((D_STATE, d_inner), F32)],
        compiler_params=pltpu.CompilerParams(
            dimension_semantics=("arbitrary", "arbitrary"), vmem_limit_bytes=VMEM_LIMIT),
        name="prompt_mixer",
    )(x, mod_p, *consts, w1_f32, w2_f32)


def _state_update_groups(groups, seq0, dec_ref, h0_ref, dtx_ref, bm_ref, cm_ref, hn_ref, yt_ref):
    bb, n_heads, p, n = h0_ref.shape
    per = n_heads // N_GROUPS
    n_seq = yt_ref.shape[1]
    lane = lax.broadcasted_iota(jnp.int32, (n, n_seq), 1)
    x_t = [dtx_ref[j].T for j in range(bb)]
    c_t = [cm_ref[j].T for j in range(bb)]
    lhs_rows, rhs_rows = [], []
    for t, g in enumerate(groups):
        lhs, rhs = [], []
        for j in range(bb):
            brow = bm_ref[j, g:g + 1, :]
            hs = []
            for r in range(per):
                h = g * per + r
                hn = dec_ref[(seq0 + j) * n_heads + h] * h0_ref[j, h] + x_t[j][:, h:h + 1] * brow
                hn_ref[j, h] = hn
                hs.append(hn.astype(BF16))
            lhs.append(jnp.concatenate(hs, axis=0))
            rhs.append(jnp.where(lane == seq0 + j, c_t[j][:, g:g + 1], 0.0).astype(BF16))
        lhs_rows.append(jnp.concatenate(lhs, axis=1))
        col = jnp.concatenate(rhs, axis=0)
        zero = jnp.zeros_like(col)
        rhs_rows.append(jnp.concatenate([col if u == t else zero for u in range(len(groups))], axis=1))
    out = jnp.dot(jnp.concatenate(lhs_rows, axis=1), jnp.concatenate(rhs_rows, axis=0),
                  preferred_element_type=F32)
    for t, g in enumerate(groups):
        rows = slice(g * per * p, (g + 1) * per * p)
        yt_ref[rows, :] += out[:, t * n_seq:(t + 1) * n_seq]


def _prompt_mlp_kernel(dec_ref, x_ref, mod_ref, n2g_ref, nfg_ref, w1_ref, w2_ref,
                       h0_ref, dtx_ref, bm_ref, cm_ref, o_ref, hn_ref, yt_ref):
    i = pl.program_id(0)
    d = x_ref.shape[1]
    bb = h0_ref.shape[0]
    n_blk = w1_ref.shape[1] // MLP_BLOCK
    sh2 = mod_ref[0, :, 3 * d:4 * d]
    sc2 = mod_ref[0, :, 4 * d:5 * d]
    g2 = mod_ref[0, :, 5 * d:6 * d]

    @pl.when(i == 0)
    def _():
        yt_ref[...] = jnp.zeros(yt_ref.shape, F32)

    def between(blk):
        if blk == n_blk - 2:
            for g in range(N_GROUPS):
                _state_update_groups((g,), i * bb, dec_ref, h0_ref, dtx_ref, bm_ref, cm_ref, hn_ref,
                                     yt_ref)

    x2 = _mlp(x_ref[...], sh2, sc2, g2, n2g_ref[...], w1_ref, w2_ref, between)
    o_ref[...] = _rms(x2, nfg_ref[...])


def _prompt_mlp_with_state(x1, mod_p, n2g, nfg, w1, w2, dec_flat, h0, dtx3, bm3, cm3, *, tm,
                           rows_per_seq):
    t, d = x1.shape
    per = rows_per_seq // tm
    steps = t // tm
    n, n_heads, p, ns = h0.shape
    assert n % steps == 0
    bb = n // steps

    def full(shape):
        nd = len(shape)
        return pl.BlockSpec(shape, lambda i, s: (0,) * nd, pipeline_mode=pl.Buffered(1))

    return pl.pallas_call(
        _prompt_mlp_kernel,
        grid_spec=pltpu.PrefetchScalarGridSpec(
            num_scalar_prefetch=1,
            grid=(steps,),
            in_specs=[pl.BlockSpec((tm, d), lambda i, s: (i, 0)),
                      pl.BlockSpec((1, 1, mod_p.shape[2]), lambda i, s: (i // per, 0, 0)),
                      full(n2g.shape), full(nfg.shape), full(w1.shape), full(w2.shape),
                      pl.BlockSpec((bb, n_heads, p, ns), lambda i, s: (i, 0, 0, 0)),
                      pl.BlockSpec((bb, n_heads, p), lambda i, s: (i, 0, 0)),
                      pl.BlockSpec((bb, N_GROUPS, ns), lambda i, s: (i, 0, 0)),
                      pl.BlockSpec((bb, N_GROUPS, ns), lambda i, s: (i, 0, 0))],
            out_specs=[pl.BlockSpec((tm, d), lambda i, s: (i, 0)),
                       pl.BlockSpec((bb, n_heads, p, ns), lambda i, s: (i, 0, 0, 0)),
                       pl.BlockSpec((n_heads * p, n), lambda i, s: (0, 0))]),
        out_shape=[jax.ShapeDtypeStruct((t, d), F32),
                   jax.ShapeDtypeStruct(h0.shape, F32),
                   jax.ShapeDtypeStruct((n_heads * p, n), F32)],
        compiler_params=pltpu.CompilerParams(
            dimension_semantics=("arbitrary",), vmem_limit_bytes=VMEM_LIMIT),
        name="prompt_mlp",
    )(dec_flat, x1, mod_p, n2g, nfg, w1, w2, h0, dtx3, bm3, cm3)


def _sample_pre_kernel(row0, x_ref, mod_ref, sa_ref, sbc_ref, n1g_ref, w_abc_ref, w_z_ref, w_xbc_ref,
                       w_dt_ref, w_g_ref, caw_ref, cbw_ref, cbb_ref, dtb_ref, alog_ref, w_ao_ref,
                       e2_ref,
                       ma_ref, na_ref, nbc_ref, dtx_ref, xs_ref, bm_ref, cm_ref, dec_ref, sz_ref,
                       sgb_ref):
    d = x_ref.shape[1]
    d_inner = w_z_ref.shape[1]
    ka = caw_ref.shape[0]
    kb = cbw_ref.shape[0]
    x = x_ref[...]
    rows = slice(row0, row0 + x.shape[0])
    sh1 = mod_ref[rows, 0:d]
    sc1 = mod_ref[rows, d:2 * d]
    ub = (_rms(x, n1g_ref[...]) * (1.0 + sc1) + sh1).astype(BF16)

    bgate = jnp.dot(ub, w_abc_ref[:, 0:d], preferred_element_type=F32)
    cgate = jnp.dot(ub, w_abc_ref[:, d:2 * d], preferred_element_type=F32)
    hval = jnp.dot(ub, w_abc_ref[:, 2 * d:3 * d], preferred_element_type=F32)
    ch = cgate * hval
    conv = ch * caw_ref[ka - 1:ka, :]
    for k in range(ka - 1):
        conv = conv + sa_ref[k] * caw_ref[k:k + 1, :]
    for k in range(ka - 2):
        na_ref[k] = sa_ref[k + 1]
    na_ref[ka - 2] = ch
    y_a = _bdot(bgate * conv, w_ao_ref[...])
    ma_ref[...] = jax.nn.sigmoid(jnp.dot(ub, w_g_ref[:, 0:d], preferred_element_type=F32)) * y_a
    sgb_ref[...] = jax.nn.sigmoid(jnp.dot(ub, w_g_ref[:, d:2 * d], preferred_element_type=F32))

    xbc = jnp.dot(ub, w_xbc_ref[...], preferred_element_type=F32)
    xc = xbc * cbw_ref[kb - 1:kb, :]
    for k in range(kb - 1):
        xc = xc + sbc_ref[k] * cbw_ref[k:k + 1, :]
    xc = _silu(xc + cbb_ref[...])
    for k in range(kb - 2):
        nbc_ref[k] = sbc_ref[k + 1]
    nbc_ref[kb - 2] = xbc

    gn = N_GROUPS * D_STATE
    xs = xc[:, 0:d_inner]
    dt = _softplus(jnp.dot(ub, w_dt_ref[...], preferred_element_type=F32) + dtb_ref[...])
    dec_ref[...] = jnp.exp(dt * (-jnp.exp(alog_ref[...])))
    hi, mid, lo = _split3(dt)
    e3 = e2_ref[0:LANES, :]
    dt_x = (jnp.dot(hi, e3, preferred_element_type=F32) + jnp.dot(mid, e3, preferred_element_type=F32)
            + jnp.dot(lo, e3, preferred_element_type=F32))
    dtx_ref[...] = xs * dt_x
    xs_ref[...] = xs
    bm_ref[...] = xc[:, d_inner:d_inner + gn]
    cm_ref[...] = xc[:, d_inner + gn:d_inner + 2 * gn]
    sz_ref[...] = _silu(jnp.dot(ub, w_z_ref[...], preferred_element_type=F32))


def _sample_pre(x, mod, row0, sa, sbc, n1g, w_abc, w_z, w_xbc, w_dt, w_g, caw, cbw, cbb, dtb, alog, w_ao, e2):
    n, d = x.shape
    d_inner = w_z.shape[1]
    gn = N_GROUPS * D_STATE
    shapes = [(n, d), sa.shape, sbc.shape, (n, d_inner), (n, d_inner), (n, gn), (n, gn), (n, LANES),
              (n, d_inner), (n, d)]
    return pl.pallas_call(
        functools.partial(_sample_pre_kernel, row0),
        out_shape=[jax.ShapeDtypeStruct(s, F32) for s in shapes],
        compiler_params=pltpu.CompilerParams(vmem_limit_bytes=VMEM_LIMIT),
        name="sample_pre",
    )(x, mod, sa, sbc, n1g, w_abc, w_z, w_xbc, w_dt, w_g, caw, cbw, cbb, dtb, alog, w_ao, e2)


def _sample_post_kernel(row0, x_ref, mod_ref, yt_ref, xs_ref, sz_ref, ma_ref, sgb_ref, dexp_ref, sng_ref,
                        w_bo_ref, w_o_ref, n2g_ref, nfg_ref, w1_ref, w2_ref, o_ref):
    d = x_ref.shape[1]
    rows = slice(row0, row0 + x_ref.shape[0])
    g1 = mod_ref[rows, 2 * d:3 * d]
    sh2 = mod_ref[rows, 3 * d:4 * d]
    sc2 = mod_ref[rows, 4 * d:5 * d]
    g2 = mod_ref[rows, 5 * d:6 * d]
    ys = (yt_ref[...].T + dexp_ref[...] * xs_ref[...]) * sz_ref[...]
    y_b = _bdot(_group_rmsnorm(ys, sng_ref[...]), w_bo_ref[...])
    merged = ma_ref[...] + sgb_ref[...] * y_b
    x1 = x_ref[...] + g1 * _bdot(merged, w_o_ref[...])
    x2 = _mlp(x1, sh2, sc2, g2, n2g_ref[...], w1_ref, w2_ref)
    o_ref[...] = _rms(x2, nfg_ref[...])


def _sample_post(x, mod, row0, y, xs, sz, ma, sgb, dexp, sng, w_bo, w_o, n2g, nfg, w1, w2):
    return pl.pallas_call(
        functools.partial(_sample_post_kernel, row0),
        out_shape=jax.ShapeDtypeStruct(x.shape, F32),
        compiler_params=pltpu.CompilerParams(vmem_limit_bytes=VMEM_LIMIT),
        name="sample_post",
    )(x, mod, y, xs, sz, ma, sgb, dexp, sng, w_bo, w_o, n2g, nfg, w1, w2)


def _transpose_cast_kernel(w_ref, o_ref):
    o_ref[...] = w_ref[0].T.astype(BF16)


def _transposed_section(w_t, layer, row0, rows):
    _, _, k = w_t.shape
    rb = min(rows, 512)
    assert rows % rb == 0
    return pl.pallas_call(
        _transpose_cast_kernel,
        grid=(rows // rb,),
        in_specs=[pl.BlockSpec((pl.Element(1), pl.Element(rb), pl.Element(k)),
                               lambda i: (layer, pl.multiple_of(row0 + i * rb, 8), 0))],
        out_specs=pl.BlockSpec((k, rb), lambda i: (0, i)),
        out_shape=jax.ShapeDtypeStruct((k, rows), BF16),
        compiler_params=pltpu.CompilerParams(
            dimension_semantics=("arbitrary",), vmem_limit_bytes=VMEM_LIMIT),
        name="w_in_section",
    )(w_t)


def _layer_weights(l, w_ada, b_ada, norm1_g, w_in, conv_a_w, w_a_out, conv_b_w, conv_b_b, dt_bias,
                   a_log, d_skip, ssm_norm_g, w_b_out, w_o, norm2_g, w_mlp1, w_mlp2):
    d = w_in.shape[1]
    d_inner = w_b_out.shape[1]
    d_xbc = conv_b_w.shape[2]
    n_heads = dt_bias.shape[1]
    o_z = 3 * d
    o_xbc = o_z + d_inner
    o_dt = o_xbc + d_xbc
    o_g = o_dt + n_heads
    row = lambda v: v.reshape(1, -1)
    lane_pad = lambda v: jnp.pad(v, ((0, 0), (0, LANES - v.shape[1])))
    w_in_t = jnp.transpose(w_in, (0, 2, 1))
    w_abc = _transposed_section(w_in_t, l, 0, o_z)
    w_z = _transposed_section(w_in_t, l, o_z, o_xbc - o_z)
    w_xbc = _transposed_section(w_in_t, l, o_xbc, o_dt - o_xbc)
    w_dt = lane_pad(_transposed_section(w_in_t, l, o_dt, o_g - o_dt))
    w_g = _transposed_section(w_in_t, l, o_g, 2 * d)
    return dict(
        w_ada=w_ada[l], b_ada=row(b_ada[l]), n1g=row(norm1_g[l]),
        w_abc=w_abc, w_z=w_z, w_xbc=w_xbc, w_dt=w_dt, w_g=w_g,
        caw=conv_a_w[l], cbw=conv_b_w[l], cbb=row(conv_b_b[l]),
        dtb=lane_pad(row(dt_bias[l])), alog=lane_pad(row(a_log[l])),
        dexp=row(jnp.repeat(d_skip[l], HEAD_DIM)), sng=row(ssm_norm_g[l]),
        w_ao=w_a_out[l].astype(BF16), w_bo=w_b_out[l].astype(BF16), w_o=w_o[l].astype(BF16),
        n2g=row(norm2_g[l]), w1_f32=w_mlp1[l], w2_f32=w_mlp2[l])


def kernel(x_prompt, x_sample, c_prompt, c_sample, state_shortconv, state_ssm_conv, state_ssm, w_ada, b_ada, norm1_g, w_in, conv_a_w, w_a_out, conv_b_w, conv_b_b, dt_bias, a_log, d_skip, ssm_norm_g, w_b_out, w_o, norm2_g, w_mlp1, w_mlp2, norm_f_g):
    bp, lp, d = x_prompt.shape
    bs, ls, _ = x_sample.shape
    assert ls == 1
    depth = w_in.shape[0]
    d_inner = w_b_out.shape[1]
    n_heads = dt_bias.shape[1]
    assert n_heads <= LANES and d_inner == n_heads * HEAD_DIM

    e1 = (jnp.arange(LANES)[:, None] == (jnp.arange(d_inner)[None, :] // HEAD_DIM)).astype(BF16)
    e2 = jnp.concatenate([e1, e1], axis=0)
    nfg = norm_f_g.reshape(1, -1)

    assert depth == 1
    w = _layer_weights(0, w_ada, b_ada, norm1_g, w_in, conv_a_w, w_a_out, conv_b_w, conv_b_b,
                       dt_bias, a_log, d_skip, ssm_norm_g, w_b_out, w_o, norm2_g, w_mlp1, w_mlp2)
    mod = _modulation(jnp.concatenate([c_prompt, c_sample], axis=0), w["w_ada"], w["b_ada"])
    mod_p = mod[:bp].reshape(bp, 1, -1)

    xs0 = x_sample.reshape(bs, d)
    sa = jnp.transpose(state_shortconv[0], (1, 0, 2))
    sbc = jnp.transpose(state_ssm_conv[0], (1, 0, 2))
    ma, na_s, nbc_s, dtx, xs, bm, cm, dec, sz, sgb = _sample_pre(
        xs0, mod, bp, sa, sbc, w["n1g"], w["w_abc"], w["w_z"], w["w_xbc"], w["w_dt"], w["w_g"],
        w["caw"], w["cbw"], w["cbb"], w["dtb"], w["alog"], w["w_ao"], e2)

    x1, na, nbc, nssm, w1, w2 = _prompt_mixer(
        x_prompt, mod_p, w["n1g"], w["w_abc"], w["w_z"], w["w_xbc"], w["w_dt"], w["w_g"], w["caw"],
        w["cbw"], w["cbb"], w["dtb"], w["alog"], w["dexp"], w["sng"], w["w_ao"], w["w_bo"],
        w["w_o"], e2, w["w1_f32"], w["w2_f32"], tq=256)
    yp, hn, yt = _prompt_mlp_with_state(
        x1.reshape(bp * lp, d), mod_p, w["n2g"], nfg, w1, w2,
        dec[:, :n_heads].reshape(-1), state_ssm[0], dtx.reshape(bs, n_heads, HEAD_DIM),
        bm.reshape(bs, N_GROUPS, D_STATE), cm.reshape(bs, N_GROUPS, D_STATE),
        tm=512, rows_per_seq=lp)

    ys = _sample_post(xs0, mod, bp, yt, xs, sz, ma, sgb, w["dexp"], w["sng"],
                      w["w_bo"], w["w_o"], w["n2g"], nfg, w1, w2)

    return (yp.reshape(bp, lp, d), ys.reshape(bs, ls, d), na[None], nbc[None], nssm[None],
            jnp.transpose(na_s, (1, 0, 2))[None], jnp.transpose(nbc_s, (1, 0, 2))[None], hn[None])
```

```python
import functools

import jax
import jax.numpy as jnp
from jax import lax
from jax.experimental import pallas as pl
from jax.experimental.pallas import tpu as pltpu

F32 = jnp.float32
BF16 = jnp.bfloat16

EPS = 1e-6
LOG2E = 1.4426950408889634
HEAD_DIM = 64
N_GROUPS = 4
D_STATE = 128
LANES = 128
SSD_CHUNK = 128
QUAD = 4
MLP_BLOCK = 1024
VMEM_LIMIT = 56 * 1024 * 1024


def _full_spec(shape):
    nd = len(shape)
    return pl.BlockSpec(shape, lambda *_: (0,) * nd, pipeline_mode=pl.Buffered(1))


def _bdot(a, b):
    return jnp.dot(a.astype(BF16), b, preferred_element_type=F32)


def _dot_nt(a, b):
    return lax.dot_general(a, b, (((1,), (1,)), ((), ())), preferred_element_type=F32)


def _rms(x, g):
    ms = jnp.mean(x * x, axis=-1, keepdims=True)
    return x * lax.rsqrt(ms + EPS) * g


def _silu(x):
    return x * jax.nn.sigmoid(x)


def _softplus(x):
    return jnp.maximum(x, 0.0) + jnp.log1p(jnp.exp(-jnp.abs(x)))


def _split2(v):
    hi = v.astype(BF16)
    lo = (v - hi.astype(F32)).astype(BF16)
    return hi, lo


def _split3(v):
    hi = v.astype(BF16)
    r = v - hi.astype(F32)
    mid = r.astype(BF16)
    lo = (r - mid.astype(F32)).astype(BF16)
    return hi, mid, lo


def _expand_heads(v, e2_ref):
    hi, lo = _split2(v)
    return jnp.dot(jnp.concatenate([hi, lo], axis=1), e2_ref[...], preferred_element_type=F32)


def _group_rmsnorm(y, g):
    d = y.shape[-1]
    w = d // N_GROUPS
    parts = []
    for k in range(N_GROUPS):
        yk = y[:, k * w:(k + 1) * w]
        ms = jnp.mean(yk * yk, axis=-1, keepdims=True)
        parts.append(yk * lax.rsqrt(ms + EPS))
    return jnp.concatenate(parts, axis=1) * g


def _mlp(x, sh2, sc2, g2, n2g, w1_ref, w2_ref, between=None):
    d_ff = w1_ref.shape[1]
    blk = MLP_BLOCK
    ub = (_rms(x, n2g) * (1.0 + sc2) + sh2).astype(BF16)
    acc = None
    for j in range(d_ff // blk):
        if between is not None:
            between(j)
        h = jnp.dot(ub, w1_ref[:, j * blk:(j + 1) * blk], preferred_element_type=F32)
        h = jnp.square(jnp.maximum(h, 0.0))
        p = _bdot(h, w2_ref[j * blk:(j + 1) * blk, :])
        acc = p if acc is None else acc + p
    if between is not None:
        between(d_ff // blk)
    return x + g2 * acc


def _mod_kernel(c_ref, w_ref, b_ref, o_ref):
    c = c_ref[...]
    o_ref[...] = _bdot(_silu(c), w_ref[...].astype(BF16)) + b_ref[...]


def _modulation(c_all, w_ada, b_ada):
    n, d = c_all.shape
    dm = w_ada.shape[1]
    blk = 1024
    return pl.pallas_call(
        _mod_kernel,
        grid=(dm // blk,),
        in_specs=[pl.BlockSpec((n, d), lambda j: (0, 0)),
                  pl.BlockSpec((d, blk), lambda j: (0, j)),
                  pl.BlockSpec((1, blk), lambda j: (0, j))],
        out_specs=pl.BlockSpec((n, blk), lambda j: (0, j)),
        out_shape=jax.ShapeDtypeStruct((n, dm), F32),
        name="modulation",
    )(c_all, w_ada, b_ada)


def _tile_causal_conv(tail_ref, cur, w_ref):
    k_w = w_ref.shape[0]
    tq = cur.shape[0]
    sl = tail_ref.shape[0]
    row = lax.broadcasted_iota(jnp.int32, (sl, cur.shape[1]), 0)
    tail = tail_ref[...]
    out = cur * w_ref[k_w - 1:k_w, :]
    for k in range(k_w - 1):
        lag = k_w - 1 - k
        rolled = pltpu.roll(cur, lag, axis=0)
        head = jnp.where(row < lag, pltpu.roll(tail, lag, axis=0), rolled[0:sl])
        out = out + jnp.concatenate([head, rolled[sl:]], axis=0) * w_ref[k:k + 1, :]
    tail_ref[...] = cur[tq - sl:tq]
    return out, cur[tq - (k_w - 1):tq]


def _ssd_chunk(xs_c, b_c, c_c, dt_c, a_row, d_row, ht_ref, e2_ref):
    q = xs_c.shape[0]
    n = D_STATE
    gw = xs_c.shape[1] // N_GROUPS
    qw = QUAD * HEAD_DIM
    row = lax.broadcasted_iota(jnp.int32, (q, q), 0)
    col = lax.broadcasted_iota(jnp.int32, (q, q), 1)
    causal = row >= col
    tri = jnp.where(causal, 1.0, 0.0).astype(BF16)
    lane_head = lax.broadcasted_iota(jnp.int32, (q, qw), 1) // HEAD_DIM
    head_mask = [jnp.where(lane_head == r, 1.0, 0.0).astype(BF16) for r in range(QUAD)]

    da = dt_c * (a_row * LOG2E)
    acs = sum(jnp.dot(tri, part, preferred_element_type=F32) for part in _split3(da))
    acs_t = acs.T
    dt_t = dt_c.T
    ea = jnp.exp2(acs)
    dd = jnp.exp2(acs[q - 1:q, :] - acs) * dt_c
    ea_x = _expand_heads(ea, e2_ref)
    dd_x = _expand_heads(dd, e2_ref)
    xd = (xs_c * dd_x).astype(BF16)
    bb = b_c.astype(BF16)
    cb_ = c_c.astype(BF16)

    y_parts = []
    for g in range(N_GROUPS):
        bg = bb[:, g * n:(g + 1) * n]
        cg = cb_[:, g * n:(g + 1) * n]
        cbm = _dot_nt(cg, bg)
        ht_g = ht_ref[:, g * gw:(g + 1) * gw]
        y_off = jnp.dot(cg, ht_g.astype(BF16), preferred_element_type=F32)
        for qd in range(gw // qw):
            lo = g * gw + qd * qw
            h0 = lo // HEAD_DIM
            ws = []
            for r in range(QUAD):
                h = h0 + r
                seg = acs[:, h:h + 1] - acs_t[h:h + 1, :]
                lm = jnp.exp2(jnp.where(causal, seg, -jnp.inf))
                ws.append((cbm * lm * dt_t[h:h + 1, :]).astype(BF16))
            wcat = jnp.concatenate(ws, axis=1)
            xq = xs_c[:, lo:lo + qw]
            xq_b = xq.astype(BF16)
            rhs = jnp.concatenate([xq_b * head_mask[r] for r in range(QUAD)], axis=0)
            y_diag = jnp.dot(wcat, rhs, preferred_element_type=F32)
            y_parts.append(y_diag + y_off[:, qd * qw:(qd + 1) * qw] * ea_x[:, lo:lo + qw]
                           + d_row[:, lo:lo + qw] * xq)
        bg_t = b_c[:, g * n:(g + 1) * n].T.astype(BF16)
        st = jnp.dot(bg_t, xd[:, g * gw:(g + 1) * gw], preferred_element_type=F32)
        ht_ref[:, g * gw:(g + 1) * gw] = ht_g * ea_x[q - 1:q, g * gw:(g + 1) * gw] + st
    return jnp.concatenate(y_parts, axis=1)


def _prompt_mixer_kernel(x_ref, mod_ref, n1g_ref, w_abc_ref, w_z_ref, w_xbc_ref, w_dt_ref, w_g_ref,
                         caw_ref, cbw_ref, cbb_ref, dtb_ref, alog_ref, dexp_ref, sng_ref,
                         w_ao_ref, w_bo_ref, w_o_ref, e2_ref, w1f_ref, w2f_ref,
                         x1_ref, na_ref, nbc_ref, nssm_ref, w1_ref, w2_ref,
                         cbuf, xbuf, ybuf, ht_ref):
    w1_ref[...] = w1f_ref[...].astype(BF16)
    w2_ref[...] = w2f_ref[...].astype(BF16)

    t = pl.program_id(1)
    nt = pl.num_programs(1)
    tq = x_ref.shape[1]
    d = x_ref.shape[2]
    d_inner = w_z_ref.shape[1]

    @pl.when(t == 0)
    def _():
        cbuf[...] = jnp.zeros(cbuf.shape, F32)
        xbuf[...] = jnp.zeros(xbuf.shape, F32)
        ht_ref[...] = jnp.zeros(ht_ref.shape, F32)

    x = x_ref[0]
    sh1 = mod_ref[0, :, 0:d]
    sc1 = mod_ref[0, :, d:2 * d]
    g1 = mod_ref[0, :, 2 * d:3 * d]
    ub = (_rms(x, n1g_ref[...]) * (1.0 + sc1) + sh1).astype(BF16)

    def proj(w_ref, lo=None, hi=None):
        w = w_ref[...] if lo is None else w_ref[:, lo:hi]
        return jnp.dot(ub, w, preferred_element_type=F32)

    xbc = proj(w_xbc_ref)
    dt = _softplus(proj(w_dt_ref) + dtb_ref[...])
    a_row = -jnp.exp(alog_ref[...])
    xc, new_bc = _tile_causal_conv(xbuf, xbc, cbw_ref)
    cgate = proj(w_abc_ref, d, 2 * d)
    hval = proj(w_abc_ref, 2 * d, 3 * d)
    xc = _silu(xc + cbb_ref[...])
    bgate = proj(w_abc_ref, 0, d)
    conv, new_a = _tile_causal_conv(cbuf, cgate * hval, caw_ref)

    gn = N_GROUPS * D_STATE
    fillers = [lambda: proj(w_z_ref), lambda: _bdot(bgate * conv, w_ao_ref[...])]
    filled = []
    for c in range(tq // SSD_CHUNK):
        s = slice(c * SSD_CHUNK, (c + 1) * SSD_CHUNK)
        ybuf[s, :] = _ssd_chunk(xc[s, 0:d_inner], xc[s, d_inner:d_inner + gn],
                                xc[s, d_inner + gn:d_inner + 2 * gn], dt[s, :], a_row,
                                dexp_ref[...], ht_ref, e2_ref)
        if c < len(fillers):
            filled.append(fillers[c]())
    filled += [f() for f in fillers[len(filled):]]
    z, y_a = filled

    merged = jax.nn.sigmoid(proj(w_g_ref, 0, d)) * y_a
    y_b = _bdot(_group_rmsnorm(ybuf[...] * _silu(z), sng_ref[...]), w_bo_ref[...])
    merged = merged + jax.nn.sigmoid(proj(w_g_ref, d, 2 * d)) * y_b
    x1_ref[0] = x + g1 * _bdot(merged, w_o_ref[...])

    @pl.when(t == nt - 1)
    def _():
        na_ref[0] = new_a
        nbc_ref[0] = new_bc
        h = ht_ref[...].T
        nssm_ref[0] = h.reshape(nssm_ref.shape[1:])


def _prompt_mixer(x, mod_p, n1g, w_abc, w_z, w_xbc, w_dt, w_g, caw, cbw, cbb, dtb, alog, dexp, sng,
                  w_ao, w_bo, w_o, e2, w1_f32, w2_f32, *, tq):
    b, l, d = x.shape
    nt = l // tq
    steps = b * nt
    r1 = w1_f32.shape[0] // steps
    r2 = w2_f32.shape[0] // steps
    assert r1 * steps == w1_f32.shape[0] and r2 * steps == w2_f32.shape[0] and r1 % 16 == 0
    d_inner = w_z.shape[1]
    d_xbc = w_xbc.shape[1]
    n_heads = d_inner // HEAD_DIM
    consts = (n1g, w_abc, w_z, w_xbc, w_dt, w_g, caw, cbw, cbb, dtb, alog, dexp, sng, w_ao, w_bo, w_o, e2)
    return pl.pallas_call(
        _prompt_mixer_kernel,
        grid=(b, l // tq),
        in_specs=[pl.BlockSpec((1, tq, d), lambda i, t: (i, t, 0)),
                  pl.BlockSpec((1, 1, mod_p.shape[2]), lambda i, t: (i, 0, 0))]
                 + [_full_spec(c.shape) for c in consts]
                 + [pl.BlockSpec((r1, w1_f32.shape[1]), lambda i, t: (i * nt + t, 0)),
                    pl.BlockSpec((r2, w2_f32.shape[1]), lambda i, t: (i * nt + t, 0))],
        out_specs=[pl.BlockSpec((1, tq, d), lambda i, t: (i, t, 0)),
                   pl.BlockSpec((1, caw.shape[0] - 1, d), lambda i, t: (i, 0, 0)),
                   pl.BlockSpec((1, cbw.shape[0] - 1, d_xbc), lambda i, t: (i, 0, 0)),
                   pl.BlockSpec((1, n_heads, HEAD_DIM, D_STATE), lambda i, t: (i, 0, 0, 0)),
                   pl.BlockSpec((r1, w1_f32.shape[1]), lambda i, t: (i * nt + t, 0)),
                   pl.BlockSpec((r2, w2_f32.shape[1]), lambda i, t: (i * nt + t, 0))],
        out_shape=[jax.ShapeDtypeStruct((b, l, d), F32),
                   jax.ShapeDtypeStruct((b, caw.shape[0] - 1, d), F32),
                   jax.ShapeDtypeStruct((b, cbw.shape[0] - 1, d_xbc), F32),
                   jax.ShapeDtypeStruct((b, n_heads, HEAD_DIM, D_STATE), F32),
                   jax.ShapeDtypeStruct(w1_f32.shape, BF16),
                   jax.ShapeDtypeStruct(w2_f32.shape, BF16)],
        scratch_shapes=[pltpu.VMEM((8, d), F32),
                        pltpu.VMEM((8, d_xbc), F32),
                        pltpu.VMEM((tq, d_inner), F32),
                        pltpu.VMEM((D_STATE, d_inner), F32)],
        compiler_params=pltpu.CompilerParams(
            dimension_semantics=("arbitrary", "arbitrary"), vmem_limit_bytes=VMEM_LIMIT),
        name="prompt_mixer",
    )(x, mod_p, *consts, w1_f32, w2_f32)


def _state_update_groups(groups, seq0, dec_ref, h0_ref, dtx_ref, bm_ref, cm_ref, hn_ref, yt_ref):
    bb, n_heads, p, n = h0_ref.shape
    per = n_heads // N_GROUPS
    n_seq = yt_ref.shape[1]
    lane = lax.broadcasted_iota(jnp.int32, (n, n_seq), 1)
    x_t = [dtx_ref[j].T for j in range(bb)]
    c_t = [cm_ref[j].T for j in range(bb)]
    lhs_rows, rhs_rows = [], []
    for t, g in enumerate(groups):
        lhs, rhs = [], []
        for j in range(bb):
            brow = bm_ref[j, g:g + 1, :]
            hs = []
            for r in range(per):
                h = g * per + r
                hn = dec_ref[(seq0 + j) * n_heads + h] * h0_ref[j, h] + x_t[j][:, h:h + 1] * brow
                hn_ref[j, h] = hn
                hs.append(hn.astype(BF16))
            lhs.append(jnp.concatenate(hs, axis=0))
            rhs.append(jnp.where(lane == seq0 + j, c_t[j][:, g:g + 1], 0.0).astype(BF16))
        lhs_rows.append(jnp.concatenate(lhs, axis=1))
        col = jnp.concatenate(rhs, axis=0)
        zero = jnp.zeros_like(col)
        rhs_rows.append(jnp.concatenate([col if u == t else zero for u in range(len(groups))], axis=1))
    out = jnp.dot(jnp.concatenate(lhs_rows, axis=1), jnp.concatenate(rhs_rows, axis=0),
                  preferred_element_type=F32)
    for t, g in enumerate(groups):
        rows = slice(g * per * p, (g + 1) * per * p)
        yt_ref[rows, :] += out[:, t * n_seq:(t + 1) * n_seq]


def _prompt_mlp_kernel(dec_ref, x_ref, mod_ref, n2g_ref, nfg_ref, w1_ref, w2_ref,
                       h0_ref, dtx_ref, bm_ref, cm_ref, o_ref, hn_ref, yt_ref):
    i = pl.program_id(0)
    d = x_ref.shape[1]
    bb = h0_ref.shape[0]
    n_blk = w1_ref.shape[1] // MLP_BLOCK
    sh2 = mod_ref[0, :, 3 * d:4 * d]
    sc2 = mod_ref[0, :, 4 * d:5 * d]
    g2 = mod_ref[0, :, 5 * d:6 * d]

    @pl.when(i == 0)
    def _():
        yt_ref[...] = jnp.zeros(yt_ref.shape, F32)

    def between(blk):
        if blk == n_blk - 2:
            for g in range(N_GROUPS):
                _state_update_groups((g,), i * bb, dec_ref, h0_ref, dtx_ref, bm_ref, cm_ref, hn_ref,
                                     yt_ref)

    x2 = _mlp(x_ref[...], sh2, sc2, g2, n2g_ref[...], w1_ref, w2_ref, between)
    o_ref[...] = _rms(x2, nfg_ref[...])


def _prompt_mlp_with_state(x1, mod_p, n2g, nfg, w1, w2, dec_flat, h0, dtx3, bm3, cm3, *, tm,
                           rows_per_seq):
    t, d = x1.shape
    per = rows_per_seq // tm
    steps = t // tm
    n, n_heads, p, ns = h0.shape
    assert n % steps == 0
    bb = n // steps

    def full(shape):
        nd = len(shape)
        return pl.BlockSpec(shape, lambda i, s: (0,) * nd, pipeline_mode=pl.Buffered(1))

    return pl.pallas_call(
        _prompt_mlp_kernel,
        grid_spec=pltpu.PrefetchScalarGridSpec(
            num_scalar_prefetch=1,
            grid=(steps,),
            in_specs=[pl.BlockSpec((tm, d), lambda i, s: (i, 0)),
                      pl.BlockSpec((1, 1, mod_p.shape[2]), lambda i, s: (i // per, 0, 0)),
                      full(n2g.shape), full(nfg.shape), full(w1.shape), full(w2.shape),
                      pl.BlockSpec((bb, n_heads, p, ns), lambda i, s: (i, 0, 0, 0)),
                      pl.BlockSpec((bb, n_heads, p), lambda i, s: (i, 0, 0)),
                      pl.BlockSpec((bb, N_GROUPS, ns), lambda i, s: (i, 0, 0)),
                      pl.BlockSpec((bb, N_GROUPS, ns), lambda i, s: (i, 0, 0))],
            out_specs=[pl.BlockSpec((tm, d), lambda i, s: (i, 0)),
                       pl.BlockSpec((bb, n_heads, p, ns), lambda i, s: (i, 0, 0, 0)),
                       pl.BlockSpec((n_heads * p, n), lambda i, s: (0, 0))]),
        out_shape=[jax.ShapeDtypeStruct((t, d), F32),
                   jax.ShapeDtypeStruct(h0.shape, F32),
                   jax.ShapeDtypeStruct((n_heads * p, n), F32)],
        compiler_params=pltpu.CompilerParams(
            dimension_semantics=("arbitrary",), vmem_limit_bytes=VMEM_LIMIT),
        name="prompt_mlp",
    )(dec_flat, x1, mod_p, n2g, nfg, w1, w2, h0, dtx3, bm3, cm3)


def _sample_pre_kernel(row0, x_ref, mod_ref, sa_ref, sbc_ref, n1g_ref, w_abc_ref, w_z_ref, w_xbc_ref,
                       w_dt_ref, w_g_ref, caw_ref, cbw_ref, cbb_ref, dtb_ref, alog_ref, w_ao_ref,
                       e2_ref,
                       ma_ref, na_ref, nbc_ref, dtx_ref, xs_ref, bm_ref, cm_ref, dec_ref, sz_ref,
                       sgb_ref):
    d = x_ref.shape[1]
    d_inner = w_z_ref.shape[1]
    ka = caw_ref.shape[0]
    kb = cbw_ref.shape[0]
    x = x_ref[...]
    rows = slice(row0, row0 + x.shape[0])
    sh1 = mod_ref[rows, 0:d]
    sc1 = mod_ref[rows, d:2 * d]
    ub = (_rms(x, n1g_ref[...]) * (1.0 + sc1) + sh1).astype(BF16)

    bgate = jnp.dot(ub, w_abc_ref[:, 0:d], preferred_element_type=F32)
    cgate = jnp.dot(ub, w_abc_ref[:, d:2 * d], preferred_element_type=F32)
    hval = jnp.dot(ub, w_abc_ref[:, 2 * d:3 * d], preferred_element_type=F32)
    ch = cgate * hval
    conv = ch * caw_ref[ka - 1:ka, :]
    for k in range(ka - 1):
        conv = conv + sa_ref[k] * caw_ref[k:k + 1, :]
    for k in range(ka - 2):
        na_ref[k] = sa_ref[k + 1]
    na_ref[ka - 2] = ch
    y_a = _bdot(bgate * conv, w_ao_ref[...])
    ma_ref[...] = jax.nn.sigmoid(jnp.dot(ub, w_g_ref[:, 0:d], preferred_element_type=F32)) * y_a
    sgb_ref[...] = jax.nn.sigmoid(jnp.dot(ub, w_g_ref[:, d:2 * d], preferred_element_type=F32))

    xbc = jnp.dot(ub, w_xbc_ref[...], preferred_element_type=F32)
    xc = xbc * cbw_ref[kb - 1:kb, :]
    for k in range(kb - 1):
        xc = xc + sbc_ref[k] * cbw_ref[k:k + 1, :]
    xc = _silu(xc + cbb_ref[...])
    for k in range(kb - 2):
        nbc_ref[k] = sbc_ref[k + 1]
    nbc_ref[kb - 2] = xbc

    gn = N_GROUPS * D_STATE
    xs = xc[:, 0:d_inner]
    dt = _softplus(jnp.dot(ub, w_dt_ref[...], preferred_element_type=F32) + dtb_ref[...])
    dec_ref[...] = jnp.exp(dt * (-jnp.exp(alog_ref[...])))
    hi, mid, lo = _split3(dt)
    e3 = e2_ref[0:LANES, :]
    dt_x = (jnp.dot(hi, e3, preferred_element_type=F32) + jnp.dot(mid, e3, preferred_element_type=F32)
            + jnp.dot(lo, e3, preferred_element_type=F32))
    dtx_ref[...] = xs * dt_x
    xs_ref[...] = xs
    bm_ref[...] = xc[:, d_inner:d_inner + gn]
    cm_ref[...] = xc[:, d_inner + gn:d_inner + 2 * gn]
    sz_ref[...] = _silu(jnp.dot(ub, w_z_ref[...], preferred_element_type=F32))


def _sample_pre(x, mod, row0, sa, sbc, n1g, w_abc, w_z, w_xbc, w_dt, w_g, caw, cbw, cbb, dtb, alog, w_ao, e2):
    n, d = x.shape
    d_inner = w_z.shape[1]
    gn = N_GROUPS * D_STATE
    shapes = [(n, d), sa.shape, sbc.shape, (n, d_inner), (n, d_inner), (n, gn), (n, gn), (n, LANES),
              (n, d_inner), (n, d)]
    return pl.pallas_call(
        functools.partial(_sample_pre_kernel, row0),
        out_shape=[jax.ShapeDtypeStruct(s, F32) for s in shapes],
        compiler_params=pltpu.CompilerParams(vmem_limit_bytes=VMEM_LIMIT),
        name="sample_pre",
    )(x, mod, sa, sbc, n1g, w_abc, w_z, w_xbc, w_dt, w_g, caw, cbw, cbb, dtb, alog, w_ao, e2)


def _sample_post_kernel(row0, x_ref, mod_ref, yt_ref, xs_ref, sz_ref, ma_ref, sgb_ref, dexp_ref, sng_ref,
                        w_bo_ref, w_o_ref, n2g_ref, nfg_ref, w1_ref, w2_ref, o_ref):
    d = x_ref.shape[1]
    rows = slice(row0, row0 + x_ref.shape[0])
    g1 = mod_ref[rows, 2 * d:3 * d]
    sh2 = mod_ref[rows, 3 * d:4 * d]
    sc2 = mod_ref[rows, 4 * d:5 * d]
    g2 = mod_ref[rows, 5 * d:6 * d]
    ys = (yt_ref[...].T + dexp_ref[...] * xs_ref[...]) * sz_ref[...]
    y_b = _bdot(_group_rmsnorm(ys, sng_ref[...]), w_bo_ref[...])
    merged = ma_ref[...] + sgb_ref[...] * y_b
    x1 = x_ref[...] + g1 * _bdot(merged, w_o_ref[...])
    x2 = _mlp(x1, sh2, sc2, g2, n2g_ref[...], w1_ref, w2_ref)
    o_ref[...] = _rms(x2, nfg_ref[...])


def _sample_post(x, mod, row0, y, xs, sz, ma, sgb, dexp, sng, w_bo, w_o, n2g, nfg, w1, w2):
    return pl.pallas_call(
        functools.partial(_sample_post_kernel, row0),
        out_shape=jax.ShapeDtypeStruct(x.shape, F32),
        compiler_params=pltpu.CompilerParams(vmem_limit_bytes=VMEM_LIMIT),
        name="sample_post",
    )(x, mod, y, xs, sz, ma, sgb, dexp, sng, w_bo, w_o, n2g, nfg, w1, w2)


def _transpose_cast_kernel(w_ref, o_ref):
    o_ref[...] = w_ref[0].T.astype(BF16)


def _transposed_section(w_t, layer, row0, rows):
    _, _, k = w_t.shape
    rb = min(rows, 512)
    assert rows % rb == 0
    return pl.pallas_call(
        _transpose_cast_kernel,
        grid=(rows // rb,),
        in_specs=[pl.BlockSpec((pl.Element(1), pl.Element(rb), pl.Element(k)),
                               lambda i: (layer, pl.multiple_of(row0 + i * rb, 8), 0))],
        out_specs=pl.BlockSpec((k, rb), lambda i: (0, i)),
        out_shape=jax.ShapeDtypeStruct((k, rows), BF16),
        compiler_params=pltpu.CompilerParams(
            dimension_semantics=("arbitrary",), vmem_limit_bytes=VMEM_LIMIT),
        name="w_in_section",
    )(w_t)


def _layer_weights(l, w_ada, b_ada, norm1_g, w_in, conv_a_w, w_a_out, conv_b_w, conv_b_b, dt_bias,
                   a_log, d_skip, ssm_norm_g, w_b_out, w_o, norm2_g, w_mlp1, w_mlp2):
    d = w_in.shape[1]
    d_inner = w_b_out.shape[1]
    d_xbc = conv_b_w.shape[2]
    n_heads = dt_bias.shape[1]
    o_z = 3 * d
    o_xbc = o_z + d_inner
    o_dt = o_xbc + d_xbc
    o_g = o_dt + n_heads
    row = lambda v: v.reshape(1, -1)
    lane_pad = lambda v: jnp.pad(v, ((0, 0), (0, LANES - v.shape[1])))
    w_in_t = jnp.transpose(w_in, (0, 2, 1))
    w_abc = _transposed_section(w_in_t, l, 0, o_z)
    w_z = _transposed_section(w_in_t, l, o_z, o_xbc - o_z)
    w_xbc = _transposed_section(w_in_t, l, o_xbc, o_dt - o_xbc)
    w_dt = lane_pad(_transposed_section(w_in_t, l, o_dt, o_g - o_dt))
    w_g = _transposed_section(w_in_t, l, o_g, 2 * d)
    return dict(
        w_ada=w_ada[l], b_ada=row(b_ada[l]), n1g=row(norm1_g[l]),
        w_abc=w_abc, w_z=w_z, w_xbc=w_xbc, w_dt=w_dt, w_g=w_g,
        caw=conv_a_w[l], cbw=conv_b_w[l], cbb=row(conv_b_b[l]),
        dtb=lane_pad(row(dt_bias[l])), alog=lane_pad(row(a_log[l])),
        dexp=row(jnp.repeat(d_skip[l], HEAD_DIM)), sng=row(ssm_norm_g[l]),
        w_ao=w_a_out[l].astype(BF16), w_bo=w_b_out[l].astype(BF16), w_o=w_o[l].astype(BF16),
        n2g=row(norm2_g[l]), w1_f32=w_mlp1[l], w2_f32=w_mlp2[l])


def kernel(x_prompt, x_sample, c_prompt, c_sample, state_shortconv, state_ssm_conv, state_ssm, w_ada, b_ada, norm1_g, w_in, conv_a_w, w_a_out, conv_b_w, conv_b_b, dt_bias, a_log, d_skip, ssm_norm_g, w_b_out, w_o, norm2_g, w_mlp1, w_mlp2, norm_f_g):
    bp, lp, d = x_prompt.shape
    bs, ls, _ = x_sample.shape
    assert ls == 1
    depth = w_in.shape[0]
    d_inner = w_b_out.shape[1]
    n_heads = dt_bias.shape[1]
    assert n_heads <= LANES and d_inner == n_heads * HEAD_DIM

    e1 = (jnp.arange(LANES)[:, None] == (jnp.arange(d_inner)[None, :] // HEAD_DIM)).astype(BF16)
    e2 = jnp.concatenate([e1, e1], axis=0)
    nfg = norm_f_g.reshape(1, -1)

    assert depth == 1
    w = _layer_weights(0, w_ada, b_ada, norm1_g, w_in, conv_a_w, w_a_out, conv_b_w, conv_b_b,
                       dt_bias, a_log, d_skip, ssm_norm_g, w_b_out, w_o, norm2_g, w_mlp1, w_mlp2)
    mod = _modulation(jnp.concatenate([c_prompt, c_sample], axis=0), w["w_ada"], w["b_ada"])
    mod_p = mod[:bp].reshape(bp, 1, -1)

    xs0 = x_sample.reshape(bs, d)
    sa = jnp.transpose(state_shortconv[0], (1, 0, 2))
    sbc = jnp.transpose(state_ssm_conv[0], (1, 0, 2))
    ma, na_s, nbc_s, dtx, xs, bm, cm, dec, sz, sgb = _sample_pre(
        xs0, mod, bp, sa, sbc, w["n1g"], w["w_abc"], w["w_z"], w["w_xbc"], w["w_dt"], w["w_g"],
        w["caw"], w["cbw"], w["cbb"], w["dtb"], w["alog"], w["w_ao"], e2)

    x1, na, nbc, nssm, w1, w2 = _prompt_mixer(
        x_prompt, mod_p, w["n1g"], w["w_abc"], w["w_z"], w["w_xbc"], w["w_dt"], w["w_g"], w["caw"],
        w["cbw"], w["cbb"], w["dtb"], w["alog"], w["dexp"], w["sng"], w["w_ao"], w["w_bo"],
        w["w_o"], e2, w["w1_f32"], w["w2_f32"], tq=256)
    yp, hn, yt = _prompt_mlp_with_state(
        x1.reshape(bp * lp, d), mod_p, w["n2g"], nfg, w1, w2,
        dec[:, :n_heads].reshape(-1), state_ssm[0], dtx.reshape(bs, n_heads, HEAD_DIM),
        bm.reshape(bs, N_GROUPS, D_STATE), cm.reshape(bs, N_GROUPS, D_STATE),
        tm=512, rows_per_seq=lp)

    ys = _sample_post(xs0, mod, bp, yt, xs, sz, ma, sgb, w["dexp"], w["sng"],
                      w["w_bo"], w["w_o"], w["n2g"], nfg, w1, w2)

    return (yp.reshape(bp, lp, d), ys.reshape(bs, ls, d), na[None], nbc[None], nssm[None],
            jnp.transpose(na_s, (1, 0, 2))[None], jnp.transpose(nbc_s, (1, 0, 2))[None], hn[None])
```

```python
import functools

import jax
import jax.numpy as jnp
from jax import lax
from jax.experimental import pallas as pl
from jax.experimental.pallas import tpu as pltpu

F32 = jnp.float32
BF16 = jnp.bfloat16

EPS = 1e-6
LOG2E = 1.4426950408889634
HEAD_DIM = 64
N_GROUPS = 4
D_STATE = 128
LANES = 128
SUBLANES = 8
SECTION_ROWS = 512
MOD_BLOCK = 1024
MIXER_TILE = 256
MLP_TILE = 512
SSD_CHUNK = 128
QUAD = 4
MLP_BLOCK = 1024
VMEM_LIMIT = 56 * 1024 * 1024


def _full_spec(shape):
    nd = len(shape)
    return pl.BlockSpec(shape, lambda *_: (0,) * nd, pipeline_mode=pl.Buffered(1))


def _bdot(a, b):
    return jnp.dot(a.astype(BF16), b, preferred_element_type=F32)


def _dot_nt(a, b):
    return lax.dot_general(a, b, (((1,), (1,)), ((), ())), preferred_element_type=F32)


def _rms(x, g):
    ms = jnp.mean(x * x, axis=-1, keepdims=True)
    return x * lax.rsqrt(ms + EPS) * g


def _silu(x):
    return x * jax.nn.sigmoid(x)


def _softplus(x):
    return jnp.maximum(x, 0.0) + jnp.log1p(jnp.exp(-jnp.abs(x)))


def _split2(v):
    hi = v.astype(BF16)
    lo = (v - hi.astype(F32)).astype(BF16)
    return hi, lo


def _split3(v):
    hi = v.astype(BF16)
    r = v - hi.astype(F32)
    mid = r.astype(BF16)
    lo = (r - mid.astype(F32)).astype(BF16)
    return hi, mid, lo


def _expand_heads(v, e2_ref):
    hi, lo = _split2(v)
    return jnp.dot(jnp.concatenate([hi, lo], axis=1), e2_ref[...], preferred_element_type=F32)


def _group_rmsnorm(y, g):
    d = y.shape[-1]
    w = d // N_GROUPS
    parts = []
    for k in range(N_GROUPS):
        yk = y[:, k * w:(k + 1) * w]
        ms = jnp.mean(yk * yk, axis=-1, keepdims=True)
        parts.append(yk * lax.rsqrt(ms + EPS))
    return jnp.concatenate(parts, axis=1) * g


def _mlp(x, sh2, sc2, g2, n2g, w1_ref, w2_ref, between=None):
    d_ff = w1_ref.shape[1]
    blk = MLP_BLOCK
    ub = (_rms(x, n2g) * (1.0 + sc2) + sh2).astype(BF16)
    acc = None
    for j in range(d_ff // blk):
        if between is not None:
            between(j)
        h = jnp.dot(ub, w1_ref[:, j * blk:(j + 1) * blk], preferred_element_type=F32)
        h = jnp.square(jnp.maximum(h, 0.0))
        p = _bdot(h, w2_ref[j * blk:(j + 1) * blk, :])
        acc = p if acc is None else acc + p
    if between is not None:
        between(d_ff // blk)
    return x + g2 * acc


def _mod_kernel(c_ref, w_ref, b_ref, o_ref):
    c = c_ref[...]
    o_ref[...] = _bdot(_silu(c), w_ref[...].astype(BF16)) + b_ref[...]


def _modulation(c_all, w_ada, b_ada):
    n, d = c_all.shape
    dm = w_ada.shape[1]
    blk = MOD_BLOCK
    return pl.pallas_call(
        _mod_kernel,
        grid=(dm // blk,),
        in_specs=[pl.BlockSpec((n, d), lambda j: (0, 0)),
                  pl.BlockSpec((d, blk), lambda j: (0, j)),
                  pl.BlockSpec((1, blk), lambda j: (0, j))],
        out_specs=pl.BlockSpec((n, blk), lambda j: (0, j)),
        out_shape=jax.ShapeDtypeStruct((n, dm), F32),
        name="modulation",
    )(c_all, w_ada, b_ada)


def _tile_causal_conv(tail_ref, cur, w_ref):
    k_w = w_ref.shape[0]
    tq = cur.shape[0]
    sl = tail_ref.shape[0]
    row = lax.broadcasted_iota(jnp.int32, (sl, cur.shape[1]), 0)
    tail = tail_ref[...]
    out = cur * w_ref[k_w - 1:k_w, :]
    for k in range(k_w - 1):
        lag = k_w - 1 - k
        rolled = pltpu.roll(cur, lag, axis=0)
        head = jnp.where(row < lag, pltpu.roll(tail, lag, axis=0), rolled[0:sl])
        out = out + jnp.concatenate([head, rolled[sl:]], axis=0) * w_ref[k:k + 1, :]
    tail_ref[...] = cur[tq - sl:tq]
    return out, cur[tq - (k_w - 1):tq]


def _ssd_chunk(xs_c, b_c, c_c, dt_c, a_row, d_row, ht_ref, e2_ref):
    q = xs_c.shape[0]
    n = D_STATE
    gw = xs_c.shape[1] // N_GROUPS
    qw = QUAD * HEAD_DIM
    row = lax.broadcasted_iota(jnp.int32, (q, q), 0)
    col = lax.broadcasted_iota(jnp.int32, (q, q), 1)
    causal = row >= col
    tri = jnp.where(causal, 1.0, 0.0).astype(BF16)
    lane_head = lax.broadcasted_iota(jnp.int32, (q, qw), 1) // HEAD_DIM
    head_mask = [jnp.where(lane_head == r, 1.0, 0.0).astype(BF16) for r in range(QUAD)]

    da = dt_c * (a_row * LOG2E)
    acs = sum(jnp.dot(tri, part, preferred_element_type=F32) for part in _split3(da))
    acs_t = acs.T
    dt_t = dt_c.T
    ea = jnp.exp2(acs)
    dd = jnp.exp2(acs[q - 1:q, :] - acs) * dt_c
    ea_x = _expand_heads(ea, e2_ref)
    dd_x = _expand_heads(dd, e2_ref)
    xd = (xs_c * dd_x).astype(BF16)
    bb = b_c.astype(BF16)
    cb_ = c_c.astype(BF16)

    y_parts = []
    for g in range(N_GROUPS):
        bg = bb[:, g * n:(g + 1) * n]
        cg = cb_[:, g * n:(g + 1) * n]
        cbm = _dot_nt(cg, bg)
        ht_g = ht_ref[:, g * gw:(g + 1) * gw]
        y_off = jnp.dot(cg, ht_g.astype(BF16), preferred_element_type=F32)
        for qd in range(gw // qw):
            lo = g * gw + qd * qw
            h0 = lo // HEAD_DIM
            ws = []
            for r in range(QUAD):
                h = h0 + r
                seg = acs[:, h:h + 1] - acs_t[h:h + 1, :]
                lm = jnp.exp2(jnp.where(causal, seg, -jnp.inf))
                ws.append((cbm * lm * dt_t[h:h + 1, :]).astype(BF16))
            wcat = jnp.concatenate(ws, axis=1)
            xq = xs_c[:, lo:lo + qw]
            xq_b = xq.astype(BF16)
            rhs = jnp.concatenate([xq_b * head_mask[r] for r in range(QUAD)], axis=0)
            y_diag = jnp.dot(wcat, rhs, preferred_element_type=F32)
            y_parts.append(y_diag + y_off[:, qd * qw:(qd + 1) * qw] * ea_x[:, lo:lo + qw]
                           + d_row[:, lo:lo + qw] * xq)
        bg_t = b_c[:, g * n:(g + 1) * n].T.astype(BF16)
        st = jnp.dot(bg_t, xd[:, g * gw:(g + 1) * gw], preferred_element_type=F32)
        ht_ref[:, g * gw:(g + 1) * gw] = ht_g * ea_x[q - 1:q, g * gw:(g + 1) * gw] + st
    return jnp.concatenate(y_parts, axis=1)


def _prompt_mixer_kernel(x_ref, mod_ref, n1g_ref, w_abc_ref, w_z_ref, w_xbc_ref, w_dt_ref, w_g_ref,
                         caw_ref, cbw_ref, cbb_ref, dtb_ref, alog_ref, dexp_ref, sng_ref,
                         w_ao_ref, w_bo_ref, w_o_ref, e2_ref, w1f_ref, w2f_ref,
                         x1_ref, na_ref, nbc_ref, nssm_ref, w1_ref, w2_ref,
                         cbuf, xbuf, ybuf, ht_ref):
    w1_ref[...] = w1f_ref[...].astype(BF16)
    w2_ref[...] = w2f_ref[...].astype(BF16)

    t = pl.program_id(1)
    nt = pl.num_programs(1)
    tq = x_ref.shape[1]
    d = x_ref.shape[2]
    d_inner = w_z_ref.shape[1]

    @pl.when(t == 0)
    def _():
        cbuf[...] = jnp.zeros(cbuf.shape, F32)
        xbuf[...] = jnp.zeros(xbuf.shape, F32)
        ht_ref[...] = jnp.zeros(ht_ref.shape, F32)

    x = x_ref[0]
    sh1 = mod_ref[0, :, 0:d]
    sc1 = mod_ref[0, :, d:2 * d]
    g1 = mod_ref[0, :, 2 * d:3 * d]
    ub = (_rms(x, n1g_ref[...]) * (1.0 + sc1) + sh1).astype(BF16)

    def proj(w_ref, lo=None, hi=None):
        w = w_ref[...] if lo is None else w_ref[:, lo:hi]
        return jnp.dot(ub, w, preferred_element_type=F32)

    xbc = proj(w_xbc_ref)
    dt = _softplus(proj(w_dt_ref) + dtb_ref[...])
    a_row = -jnp.exp(alog_ref[...])
    xc, new_bc = _tile_causal_conv(xbuf, xbc, cbw_ref)
    cgate = proj(w_abc_ref, d, 2 * d)
    hval = proj(w_abc_ref, 2 * d, 3 * d)
    xc = _silu(xc + cbb_ref[...])
    bgate = proj(w_abc_ref, 0, d)
    conv, new_a = _tile_causal_conv(cbuf, cgate * hval, caw_ref)

    gn = N_GROUPS * D_STATE
    fillers = [lambda: proj(w_z_ref), lambda: _bdot(bgate * conv, w_ao_ref[...])]
    filled = []
    for c in range(tq // SSD_CHUNK):
        s = slice(c * SSD_CHUNK, (c + 1) * SSD_CHUNK)
        ybuf[s, :] = _ssd_chunk(xc[s, 0:d_inner], xc[s, d_inner:d_inner + gn],
                                xc[s, d_inner + gn:d_inner + 2 * gn], dt[s, :], a_row,
                                dexp_ref[...], ht_ref, e2_ref)
        if c < len(fillers):
            filled.append(fillers[c]())
    filled += [f() for f in fillers[len(filled):]]
    z, y_a = filled

    merged = jax.nn.sigmoid(proj(w_g_ref, 0, d)) * y_a
    y_b = _bdot(_group_rmsnorm(ybuf[...] * _silu(z), sng_ref[...]), w_bo_ref[...])
    merged = merged + jax.nn.sigmoid(proj(w_g_ref, d, 2 * d)) * y_b
    x1_ref[0] = x + g1 * _bdot(merged, w_o_ref[...])

    @pl.when(t == nt - 1)
    def _():
        na_ref[0] = new_a
        nbc_ref[0] = new_bc
        h = ht_ref[...].T
        nssm_ref[0] = h.reshape(nssm_ref.shape[1:])


def _prompt_mixer(x, mod_p, n1g, w_abc, w_z, w_xbc, w_dt, w_g, caw, cbw, cbb, dtb, alog, dexp, sng,
                  w_ao, w_bo, w_o, e2, w1_f32, w2_f32, *, tq):
    b, l, d = x.shape
    nt = l // tq
    steps = b * nt
    r1 = w1_f32.shape[0] // steps
    r2 = w2_f32.shape[0] // steps
    assert r1 * steps == w1_f32.shape[0] and r2 * steps == w2_f32.shape[0] and r1 % 16 == 0
    d_inner = w_z.shape[1]
    d_xbc = w_xbc.shape[1]
    n_heads = d_inner // HEAD_DIM
    consts = (n1g, w_abc, w_z, w_xbc, w_dt, w_g, caw, cbw, cbb, dtb, alog, dexp, sng, w_ao, w_bo, w_o, e2)
    return pl.pallas_call(
        _prompt_mixer_kernel,
        grid=(b, l // tq),
        in_specs=[pl.BlockSpec((1, tq, d), lambda i, t: (i, t, 0)),
                  pl.BlockSpec((1, 1, mod_p.shape[2]), lambda i, t: (i, 0, 0))]
                 + [_full_spec(c.shape) for c in consts]
                 + [pl.BlockSpec((r1, w1_f32.shape[1]), lambda i, t: (i * nt + t, 0)),
                    pl.BlockSpec((r2, w2_f32.shape[1]), lambda i, t: (i * nt + t, 0))],
        out_specs=[pl.BlockSpec((1, tq, d), lambda i, t: (i, t, 0)),
                   pl.BlockSpec((1, caw.shape[0] - 1, d), lambda i, t: (i, 0, 0)),
                   pl.BlockSpec((1, cbw.shape[0] - 1, d_xbc), lambda i, t: (i, 0, 0)),
                   pl.BlockSpec((1, n_heads, HEAD_DIM, D_STATE), lambda i, t: (i, 0, 0, 0)),
                   pl.BlockSpec((r1, w1_f32.shape[1]), lambda i, t: (i * nt + t, 0)),
                   pl.BlockSpec((r2, w2_f32.shape[1]), lambda i, t: (i * nt + t, 0))],
        out_shape=[jax.ShapeDtypeStruct((b, l, d), F32),
                   jax.ShapeDtypeStruct((b, caw.shape[0] - 1, d), F32),
                   jax.ShapeDtypeStruct((b, cbw.shape[0] - 1, d_xbc), F32),
                   jax.ShapeDtypeStruct((b, n_heads, HEAD_DIM, D_STATE), F32),
                   jax.ShapeDtypeStruct(w1_f32.shape, BF16),
                   jax.ShapeDtypeStruct(w2_f32.shape, BF16)],
        scratch_shapes=[pltpu.VMEM((SUBLANES, d), F32),
                        pltpu.VMEM((SUBLANES, d_xbc), F32),
                        pltpu.VMEM((tq, d_inner), F32),
                        pltpu.VMEM((D_STATE, d_inner), F32)],
        compiler_params=pltpu.CompilerParams(
            dimension_semantics=("arbitrary", "arbitrary"), vmem_limit_bytes=VMEM_LIMIT),
        name="prompt_mixer",
    )(x, mod_p, *consts, w1_f32, w2_f32)


def _state_update_groups(groups, seq0, dec_ref, h0_ref, dtx_ref, bm_ref, cm_ref, hn_ref, yt_ref):
    bb, n_heads, p, n = h0_ref.shape
    per = n_heads // N_GROUPS
    n_seq = yt_ref.shape[1]
    lane = lax.broadcasted_iota(jnp.int32, (n, n_seq), 1)
    x_t = [dtx_ref[j].T for j in range(bb)]
    c_t = [cm_ref[j].T for j in range(bb)]
    lhs_rows, rhs_rows = [], []
    for t, g in enumerate(groups):
        lhs, rhs = [], []
        for j in range(bb):
            brow = bm_ref[j, g:g + 1, :]
            hs = []
            for r in range(per):
                h = g * per + r
                hn = dec_ref[seq0 + j, h] * h0_ref[j, h] + x_t[j][:, h:h + 1] * brow
                hn_ref[j, h] = hn
                hs.append(hn.astype(BF16))
            lhs.append(jnp.concatenate(hs, axis=0))
            rhs.append(jnp.where(lane == seq0 + j, c_t[j][:, g:g + 1], 0.0).astype(BF16))
        lhs_rows.append(jnp.concatenate(lhs, axis=1))
        col = jnp.concatenate(rhs, axis=0)
        zero = jnp.zeros_like(col)
        rhs_rows.append(jnp.concatenate([col if u == t else zero for u in range(len(groups))], axis=1))
    out = jnp.dot(jnp.concatenate(lhs_rows, axis=1), jnp.concatenate(rhs_rows, axis=0),
                  preferred_element_type=F32)
    for t, g in enumerate(groups):
        rows = slice(g * per * p, (g + 1) * per * p)
        yt_ref[rows, :] += out[:, t * n_seq:(t + 1) * n_seq]


def _prompt_mlp_kernel(dec_ref, x_ref, mod_ref, n2g_ref, nfg_ref, w1_ref, w2_ref,
                       h0_ref, dtx_ref, bm_ref, cm_ref, o_ref, hn_ref, yt_ref):
    i = pl.program_id(0)
    d = x_ref.shape[1]
    bb = h0_ref.shape[0]
    n_blk = w1_ref.shape[1] // MLP_BLOCK
    sh2 = mod_ref[0, :, 3 * d:4 * d]
    sc2 = mod_ref[0, :, 4 * d:5 * d]
    g2 = mod_ref[0, :, 5 * d:6 * d]

    @pl.when(i == 0)
    def _():
        yt_ref[...] = jnp.zeros(yt_ref.shape, F32)

    def between(blk):
        if blk == n_blk // 2:
            for g in range(N_GROUPS):
                _state_update_groups((g,), i * bb, dec_ref, h0_ref, dtx_ref, bm_ref, cm_ref, hn_ref,
                                     yt_ref)

    x2 = _mlp(x_ref[...], sh2, sc2, g2, n2g_ref[...], w1_ref, w2_ref, between)
    o_ref[...] = _rms(x2, nfg_ref[...])


def _prompt_mlp_with_state(x1, mod_p, n2g, nfg, w1, w2, dec, h0, dtx3, bm3, cm3, *, tm,
                           rows_per_seq):
    t, d = x1.shape
    per = rows_per_seq // tm
    steps = t // tm
    n, n_heads, p, ns = h0.shape
    assert n % steps == 0
    bb = n // steps

    def full(shape):
        nd = len(shape)
        return pl.BlockSpec(shape, lambda i, s: (0,) * nd, pipeline_mode=pl.Buffered(1))

    return pl.pallas_call(
        _prompt_mlp_kernel,
        grid_spec=pltpu.PrefetchScalarGridSpec(
            num_scalar_prefetch=1,
            grid=(steps,),
            in_specs=[pl.BlockSpec((tm, d), lambda i, s: (i, 0)),
                      pl.BlockSpec((1, 1, mod_p.shape[2]), lambda i, s: (i // per, 0, 0)),
                      full(n2g.shape), full(nfg.shape), full(w1.shape), full(w2.shape),
                      pl.BlockSpec((bb, n_heads, p, ns), lambda i, s: (i, 0, 0, 0)),
                      pl.BlockSpec((bb, n_heads, p), lambda i, s: (i, 0, 0)),
                      pl.BlockSpec((bb, N_GROUPS, ns), lambda i, s: (i, 0, 0)),
                      pl.BlockSpec((bb, N_GROUPS, ns), lambda i, s: (i, 0, 0))],
            out_specs=[pl.BlockSpec((tm, d), lambda i, s: (i, 0)),
                       pl.BlockSpec((bb, n_heads, p, ns), lambda i, s: (i, 0, 0, 0)),
                       pl.BlockSpec((n_heads * p, n), lambda i, s: (0, 0))]),
        out_shape=[jax.ShapeDtypeStruct((t, d), F32),
                   jax.ShapeDtypeStruct(h0.shape, F32),
                   jax.ShapeDtypeStruct((n_heads * p, n), F32)],
        compiler_params=pltpu.CompilerParams(
            dimension_semantics=("arbitrary",), vmem_limit_bytes=VMEM_LIMIT),
        name="prompt_mlp",
    )(dec, x1, mod_p, n2g, nfg, w1, w2, h0, dtx3, bm3, cm3)


def _sample_pre_kernel(row0, x_ref, mod_ref, sa_ref, sbc_ref, n1g_ref, w_abc_ref, w_z_ref, w_xbc_ref,
                       w_dt_ref, w_g_ref, caw_ref, cbw_ref, cbb_ref, dtb_ref, alog_ref, w_ao_ref,
                       e2_ref,
                       ma_ref, na_ref, nbc_ref, dtx_ref, xs_ref, bm_ref, cm_ref, dec_ref, sz_ref,
                       sgb_ref):
    d = x_ref.shape[1]
    d_inner = w_z_ref.shape[1]
    ka = caw_ref.shape[0]
    kb = cbw_ref.shape[0]
    x = x_ref[...]
    rows = slice(row0, row0 + x.shape[0])
    sh1 = mod_ref[rows, 0:d]
    sc1 = mod_ref[rows, d:2 * d]
    ub = (_rms(x, n1g_ref[...]) * (1.0 + sc1) + sh1).astype(BF16)

    bgate = jnp.dot(ub, w_abc_ref[:, 0:d], preferred_element_type=F32)
    cgate = jnp.dot(ub, w_abc_ref[:, d:2 * d], preferred_element_type=F32)
    hval = jnp.dot(ub, w_abc_ref[:, 2 * d:3 * d], preferred_element_type=F32)
    ch = cgate * hval
    conv = ch * caw_ref[ka - 1:ka, :]
    for k in range(ka - 1):
        conv = conv + sa_ref[k] * caw_ref[k:k + 1, :]
    for k in range(ka - 2):
        na_ref[k] = sa_ref[k + 1]
    na_ref[ka - 2] = ch
    y_a = _bdot(bgate * conv, w_ao_ref[...])
    ma_ref[...] = jax.nn.sigmoid(jnp.dot(ub, w_g_ref[:, 0:d], preferred_element_type=F32)) * y_a
    sgb_ref[...] = jax.nn.sigmoid(jnp.dot(ub, w_g_ref[:, d:2 * d], preferred_element_type=F32))

    xbc = jnp.dot(ub, w_xbc_ref[...], preferred_element_type=F32)
    xc = xbc * cbw_ref[kb - 1:kb, :]
    for k in range(kb - 1):
        xc = xc + sbc_ref[k] * cbw_ref[k:k + 1, :]
    xc = _silu(xc + cbb_ref[...])
    for k in range(kb - 2):
        nbc_ref[k] = sbc_ref[k + 1]
    nbc_ref[kb - 2] = xbc

    gn = N_GROUPS * D_STATE
    xs = xc[:, 0:d_inner]
    dt = _softplus(jnp.dot(ub, w_dt_ref[...], preferred_element_type=F32) + dtb_ref[...])
    dec_ref[...] = jnp.exp(dt * (-jnp.exp(alog_ref[...])))
    hi, mid, lo = _split3(dt)
    e3 = e2_ref[0:LANES, :]
    dt_x = (jnp.dot(hi, e3, preferred_element_type=F32) + jnp.dot(mid, e3, preferred_element_type=F32)
            + jnp.dot(lo, e3, preferred_element_type=F32))
    dtx_ref[...] = xs * dt_x
    xs_ref[...] = xs
    bm_ref[...] = xc[:, d_inner:d_inner + gn]
    cm_ref[...] = xc[:, d_inner + gn:d_inner + 2 * gn]
    sz_ref[...] = _silu(jnp.dot(ub, w_z_ref[...], preferred_element_type=F32))


def _sample_pre(x, mod, row0, sa, sbc, n1g, w_abc, w_z, w_xbc, w_dt, w_g, caw, cbw, cbb, dtb, alog, w_ao, e2):
    n, d = x.shape
    d_inner = w_z.shape[1]
    gn = N_GROUPS * D_STATE
    shapes = [(n, d), sa.shape, sbc.shape, (n, d_inner), (n, d_inner), (n, gn), (n, gn), (n, LANES),
              (n, d_inner), (n, d)]
    return pl.pallas_call(
        functools.partial(_sample_pre_kernel, row0),
        out_shape=[jax.ShapeDtypeStruct(s, F32) for s in shapes],
        compiler_params=pltpu.CompilerParams(vmem_limit_bytes=VMEM_LIMIT),
        name="sample_pre",
    )(x, mod, sa, sbc, n1g, w_abc, w_z, w_xbc, w_dt, w_g, caw, cbw, cbb, dtb, alog, w_ao, e2)


def _sample_post_kernel(row0, x_ref, mod_ref, yt_ref, xs_ref, sz_ref, ma_ref, sgb_ref, dexp_ref, sng_ref,
                        w_bo_ref, w_o_ref, n2g_ref, nfg_ref, w1_ref, w2_ref, o_ref):
    d = x_ref.shape[1]
    rows = slice(row0, row0 + x_ref.shape[0])
    g1 = mod_ref[rows, 2 * d:3 * d]
    sh2 = mod_ref[rows, 3 * d:4 * d]
    sc2 = mod_ref[rows, 4 * d:5 * d]
    g2 = mod_ref[rows, 5 * d:6 * d]
    ys = (yt_ref[...].T + dexp_ref[...] * xs_ref[...]) * sz_ref[...]
    y_b = _bdot(_group_rmsnorm(ys, sng_ref[...]), w_bo_ref[...])
    merged = ma_ref[...] + sgb_ref[...] * y_b
    x1 = x_ref[...] + g1 * _bdot(merged, w_o_ref[...])
    x2 = _mlp(x1, sh2, sc2, g2, n2g_ref[...], w1_ref, w2_ref)
    o_ref[...] = _rms(x2, nfg_ref[...])


def _sample_post(x, mod, row0, y, xs, sz, ma, sgb, dexp, sng, w_bo, w_o, n2g, nfg, w1, w2):
    return pl.pallas_call(
        functools.partial(_sample_post_kernel, row0),
        out_shape=jax.ShapeDtypeStruct(x.shape, F32),
        compiler_params=pltpu.CompilerParams(vmem_limit_bytes=VMEM_LIMIT),
        name="sample_post",
    )(x, mod, y, xs, sz, ma, sgb, dexp, sng, w_bo, w_o, n2g, nfg, w1, w2)


def _transpose_cast_kernel(w_ref, o_ref):
    t = w_ref[0].T.astype(BF16)
    extra = o_ref.shape[1] - t.shape[1]
    o_ref[...] = t if extra == 0 else jnp.concatenate([t, jnp.zeros((t.shape[0], extra), BF16)], axis=1)


def _transposed_section(w_t, layer, row0, rows):
    _, _, k = w_t.shape
    rb = min(rows, SECTION_ROWS)
    assert rows % rb == 0
    cols = max(rb, LANES)
    return pl.pallas_call(
        _transpose_cast_kernel,
        grid=(rows // rb,),
        in_specs=[pl.BlockSpec((pl.Element(1), pl.Element(rb), pl.Element(k)),
                               lambda i: (layer, pl.multiple_of(row0 + i * rb, SUBLANES), 0))],
        out_specs=pl.BlockSpec((k, cols), lambda i: (0, i)),
        out_shape=jax.ShapeDtypeStruct((k, max(rows, LANES)), BF16),
        compiler_params=pltpu.CompilerParams(
            dimension_semantics=("arbitrary",), vmem_limit_bytes=VMEM_LIMIT),
        name="w_in_section",
    )(w_t)


def _layer_weights(l, w_ada, b_ada, norm1_g, w_in, conv_a_w, w_a_out, conv_b_w, conv_b_b, dt_bias,
                   a_log, d_skip, ssm_norm_g, w_b_out, w_o, norm2_g, w_mlp1, w_mlp2):
    d = w_in.shape[1]
    d_inner = w_b_out.shape[1]
    d_xbc = conv_b_w.shape[2]
    n_heads = dt_bias.shape[1]
    o_z = 3 * d
    o_xbc = o_z + d_inner
    o_dt = o_xbc + d_xbc
    o_g = o_dt + n_heads
    row = lambda v: v.reshape(1, -1)
    lane_pad = lambda v: jnp.pad(v, ((0, 0), (0, LANES - v.shape[1])))
    w_in_t = jnp.transpose(w_in, (0, 2, 1))
    w_abc = _transposed_section(w_in_t, l, 0, o_z)
    w_z = _transposed_section(w_in_t, l, o_z, o_xbc - o_z)
    w_xbc = _transposed_section(w_in_t, l, o_xbc, o_dt - o_xbc)
    w_dt = _transposed_section(w_in_t, l, o_dt, o_g - o_dt)
    w_g = _transposed_section(w_in_t, l, o_g, 2 * d)
    return dict(
        w_ada=w_ada[l], b_ada=row(b_ada[l]), n1g=row(norm1_g[l]),
        w_abc=w_abc, w_z=w_z, w_xbc=w_xbc, w_dt=w_dt, w_g=w_g,
        caw=conv_a_w[l], cbw=conv_b_w[l], cbb=row(conv_b_b[l]),
        dtb=lane_pad(row(dt_bias[l])), alog=lane_pad(row(a_log[l])),
        dexp=row(jnp.repeat(d_skip[l], HEAD_DIM)), sng=row(ssm_norm_g[l]),
        w_ao=w_a_out[l].astype(BF16), w_bo=w_b_out[l].astype(BF16), w_o=w_o[l].astype(BF16),
        n2g=row(norm2_g[l]), w1_f32=w_mlp1[l], w2_f32=w_mlp2[l])


def kernel(x_prompt, x_sample, c_prompt, c_sample, state_shortconv, state_ssm_conv, state_ssm, w_ada, b_ada, norm1_g, w_in, conv_a_w, w_a_out, conv_b_w, conv_b_b, dt_bias, a_log, d_skip, ssm_norm_g, w_b_out, w_o, norm2_g, w_mlp1, w_mlp2, norm_f_g):
    bp, lp, d = x_prompt.shape
    bs, ls, _ = x_sample.shape
    assert ls == 1
    depth = w_in.shape[0]
    d_inner = w_b_out.shape[1]
    n_heads = dt_bias.shape[1]
    assert n_heads <= LANES and d_inner == n_heads * HEAD_DIM

    e1 = (jnp.arange(LANES)[:, None] == (jnp.arange(d_inner)[None, :] // HEAD_DIM)).astype(BF16)
    e2 = jnp.concatenate([e1, e1], axis=0)
    nfg = norm_f_g.reshape(1, -1)

    assert depth == 1
    w = _layer_weights(0, w_ada, b_ada, norm1_g, w_in, conv_a_w, w_a_out, conv_b_w, conv_b_b,
                       dt_bias, a_log, d_skip, ssm_norm_g, w_b_out, w_o, norm2_g, w_mlp1, w_mlp2)
    mod = _modulation(jnp.concatenate([c_prompt, c_sample], axis=0), w["w_ada"], w["b_ada"])
    mod_p = mod[:bp].reshape(bp, 1, -1)

    xs0 = x_sample.reshape(bs, d)
    sa = jnp.transpose(state_shortconv[0], (1, 0, 2))
    sbc = jnp.transpose(state_ssm_conv[0], (1, 0, 2))
    ma, na_s, nbc_s, dtx, xs, bm, cm, dec, sz, sgb = _sample_pre(
        xs0, mod, bp, sa, sbc, w["n1g"], w["w_abc"], w["w_z"], w["w_xbc"], w["w_dt"], w["w_g"],
        w["caw"], w["cbw"], w["cbb"], w["dtb"], w["alog"], w["w_ao"], e2)

    x1, na, nbc, nssm, w1, w2 = _prompt_mixer(
        x_prompt, mod_p, w["n1g"], w["w_abc"], w["w_z"], w["w_xbc"], w["w_dt"], w["w_g"], w["caw"],
        w["cbw"], w["cbb"], w["dtb"], w["alog"], w["dexp"], w["sng"], w["w_ao"], w["w_bo"],
        w["w_o"], e2, w["w1_f32"], w["w2_f32"], tq=MIXER_TILE)
    yp, hn, yt = _prompt_mlp_with_state(
        x1.reshape(bp * lp, d), mod_p, w["n2g"], nfg, w1, w2,
        dec, state_ssm[0], dtx.reshape(bs, n_heads, HEAD_DIM),
        bm.reshape(bs, N_GROUPS, D_STATE), cm.reshape(bs, N_GROUPS, D_STATE),
        tm=MLP_TILE, rows_per_seq=lp)

    ys = _sample_post(xs0, mod, bp, yt, xs, sz, ma, sgb, w["dexp"], w["sng"],
                      w["w_bo"], w["w_o"], w["n2g"], nfg, w1, w2)

    return (yp.reshape(bp, lp, d), ys.reshape(bs, ls, d), na[None], nbc[None], nssm[None],
            jnp.transpose(na_s, (1, 0, 2))[None], jnp.transpose(nbc_s, (1, 0, 2))[None], hn[None])
```
